```python
import math
import jax, jax.numpy as jnp
from jax import lax
import numpy as np

D_MODEL = 1024
BATCH = 2
SEQ = 8192
DEPTH = 2

N_MIXERS = 2
ATTN_HEADS = 16
ATTN_HEAD_DIM = D_MODEL // ATTN_HEADS
ROT_DIM = ATTN_HEAD_DIM // 4
ROPE_THETA = 500000.0
MOBA_BLOCK = 256
MOBA_TOPK = 3
Q_CHUNK = 64
MLSTM_HEADS = 8
MLSTM_QK_DIM = D_MODEL // (2 * MLSTM_HEADS)
MLSTM_V_DIM = D_MODEL // MLSTM_HEADS
MLSTM_CHUNK = 64
GATE_SOFTCAP = 15.0
MLSTM_IN_WIDTH = 2 * MLSTM_HEADS * MLSTM_QK_DIM + MLSTM_HEADS * MLSTM_V_DIM + D_MODEL + 2 * MLSTM_HEADS
D_FF = ((8 * D_MODEL + 3 * 256 - 1) // (3 * 256)) * 256
RMS_EPS = 1e-6

kernel_name = "hybrid_moba_mlstm_swiglu"


def rmsnorm(x, w):
    xf = x.astype(jnp.float32)
    y = xf * lax.rsqrt(jnp.mean(xf * xf, axis=-1, keepdims=True) + RMS_EPS)
    return (y * w.astype(jnp.float32)).astype(x.dtype)


def rotary_tables(seq):
    pos = jnp.arange(seq, dtype=jnp.float32)
    inv_freq = ROPE_THETA ** (-jnp.arange(0, ROT_DIM, 2, dtype=jnp.float32) / ROT_DIM)
    ang = pos[:, None] * inv_freq[None, :]
    return jnp.cos(ang), jnp.sin(ang)


def partial_rotary(x, cos, sin):
    xf = x.astype(jnp.float32)
    half = ROT_DIM // 2
    x1, x2, rest = xf[..., :half], xf[..., half:ROT_DIM], xf[..., ROT_DIM:]
    rot = jnp.concatenate([x1 * cos - x2 * sin, x2 * cos + x1 * sin, rest], axis=-1)
    return rot.astype(x.dtype)


def moba_attention(q, k, v):
    B, H, S, Dh = q.shape
    nb = -(-S // MOBA_BLOCK)
    pad = nb * MOBA_BLOCK - S
    kp = jnp.pad(k, ((0, 0), (0, 0), (0, pad), (0, 0)))
    vp = jnp.pad(v, ((0, 0), (0, 0), (0, pad), (0, 0)))
    kb = kp.reshape(B, H, nb, MOBA_BLOCK, Dh)
    vb = vp.reshape(B, H, nb, MOBA_BLOCK, Dh)
    kmean = jnp.mean(kb.astype(jnp.float32), axis=3)
    topk = min(MOBA_TOPK, nb)
    scale = Dh ** -0.5
    blk_ids = jnp.arange(nb)
    gather_blocks = jax.vmap(jax.vmap(lambda t, i: t[i]))

    def one_chunk(c):
        q0 = c * Q_CHUNK
        cur = q0 // MOBA_BLOCK
        qc = lax.dynamic_slice_in_dim(q, q0, Q_CHUNK, axis=2) * scale
        gate = jnp.einsum('bhqd,bhnd->bhqn', qc.astype(jnp.float32), kmean)
        gate = jnp.where(blk_ids[None, None, None, :] < cur, gate, -jnp.inf)
        _, sel = lax.top_k(gate, topk)
        valid = sel < cur
        kg = gather_blocks(kb, sel)
        vg = gather_blocks(vb, sel)
        s_sel = jnp.einsum('bhqd,bhqjkd->bhqjk', qc, kg, preferred_element_type=jnp.float32)
        s_sel = jnp.where(valid[..., None], s_sel, -jnp.inf).reshape(B, H, Q_CHUNK, topk * MOBA_BLOCK)
        k_own = lax.dynamic_index_in_dim(kb, cur, axis=2, keepdims=False)
        v_own = lax.dynamic_index_in_dim(vb, cur, axis=2, keepdims=False)
        s_own = jnp.einsum('bhqd,bhkd->bhqk', qc, k_own, preferred_element_type=jnp.float32)
        qpos = q0 + jnp.arange(Q_CHUNK)
        kpos = cur * MOBA_BLOCK + jnp.arange(MOBA_BLOCK)
        s_own = jnp.where(kpos[None, :] <= qpos[:, None], s_own, -jnp.inf)
        p = jax.nn.softmax(jnp.concatenate([s_sel, s_own], axis=-1), axis=-1)
        p_sel = p[..., :topk * MOBA_BLOCK].reshape(B, H, Q_CHUNK, topk, MOBA_BLOCK).astype(v.dtype)
        p_own = p[..., topk * MOBA_BLOCK:].astype(v.dtype)
        out = (jnp.einsum('bhqjk,bhqjkd->bhqd', p_sel, vg, preferred_element_type=jnp.float32)
               + jnp.einsum('bhqk,bhkd->bhqd', p_own, v_own, preferred_element_type=jnp.float32))
        return out.astype(q.dtype)

    outs = lax.map(one_chunk, jnp.arange(S // Q_CHUNK))
    return jnp.moveaxis(outs, 0, 2).reshape(B, H, S, Dh)


def moba_mixer(xn, w_qkv, w_o, cos, sin):
    B, S, _ = xn.shape
    qkv = jnp.einsum('bsd,de->bse', xn, w_qkv).reshape(B, S, 3, ATTN_HEADS, ATTN_HEAD_DIM)
    qkv = jnp.transpose(qkv, (2, 0, 3, 1, 4))
    q = partial_rotary(qkv[0], cos, sin)
    k = partial_rotary(qkv[1], cos, sin)
    o = moba_attention(q, k, qkv[2])
    o = jnp.transpose(o, (0, 2, 1, 3)).reshape(B, S, D_MODEL)
    return jnp.einsum('bsd,de->bse', o, w_o)


def mlstm_chunkwise(q, k, v, i_pre, f_pre):
    B, H, S, dk = q.shape
    dv = v.shape[-1]
    L = MLSTM_CHUNK
    NC = S // L
    qf = q.astype(jnp.float32).reshape(B, H, NC, L, dk) * (dk ** -0.5)
    kf = k.astype(jnp.float32).reshape(B, H, NC, L, dk)
    vf = v.astype(jnp.float32).reshape(B, H, NC, L, dv)
    logf = jax.nn.log_sigmoid(f_pre).reshape(B, H, NC, L)
    ig = i_pre.reshape(B, H, NC, L)
    b = jnp.cumsum(logf, axis=-1)
    bL = b[..., -1]
    causal = jnp.tril(jnp.ones((L, L), dtype=bool))
    dmat = jnp.where(causal, b[..., :, None] - b[..., None, :] + ig[..., None, :], -jnp.inf)
    m_intra = jnp.max(dmat, axis=-1)
    a = bL[..., None] - b + ig
    m_loc = jnp.max(a, axis=-1)
    w = jnp.exp(a - m_loc[..., None])
    c_loc = jnp.einsum('bhcl,bhclv,bhclk->bhcvk', w, vf, kf)
    n_loc = jnp.einsum('bhcl,bhclk->bhck', w, kf)

    def step(carry, inp):
        c_prev, n_prev, m_prev = carry
        c_l, n_l, m_l, bl = inp
        m_new = jnp.maximum(bl + m_prev, m_l)
        decay = jnp.exp(bl + m_prev - m_new)
        sc = jnp.exp(m_l - m_new)
        c_new = decay[..., None, None] * c_prev + sc[..., None, None] * c_l
        n_new = decay[..., None] * n_prev + sc[..., None] * n_l
        return (c_new, n_new, m_new), (c_prev, n_prev, m_prev)

    init = (jnp.zeros((B, H, dv, dk), jnp.float32), jnp.zeros((B, H, dk), jnp.float32),
            jnp.zeros((B, H), jnp.float32))
    xs = (jnp.moveaxis(c_loc, 2, 0), jnp.moveaxis(n_loc, 2, 0), jnp.moveaxis(m_loc, 2, 0),
          jnp.moveaxis(bL, 2, 0))
    _, (c_in, n_in, m_in) = lax.scan(step, init, xs)
    c_in = jnp.moveaxis(c_in, 0, 2)
    n_in = jnp.moveaxis(n_in, 0, 2)
    m_in = jnp.moveaxis(m_in, 0, 2)

    b_inter = b + m_in[..., None]
    m_t = jnp.maximum(b_inter, m_intra)
    inter = jnp.exp(b_inter - m_t)
    s_qk = jnp.einsum('bhcld,bhcsd->bhcls', qf, kf) * jnp.exp(dmat - m_t[..., None])
    num = (inter[..., None] * jnp.einsum('bhcld,bhcvd->bhclv', qf, c_in)
           + jnp.einsum('bhcls,bhcsv->bhclv', s_qk, vf))
    den = inter * jnp.einsum('bhcld,bhcd->bhcl', qf, n_in) + jnp.sum(s_qk, axis=-1)
    h = num / jnp.maximum(jnp.abs(den), jnp.exp(-m_t))[..., None]
    return h.reshape(B, H, S, dv)


def mlstm_mixer(xn, w_in, b_gates, head_norm, w_out):
    B, S, _ = xn.shape
    H, dk, dv = MLSTM_HEADS, MLSTM_QK_DIM, MLSTM_V_DIM
    proj = jnp.einsum('bsd,de->bse', xn, w_in)
    q_end = H * dk
    k_end = 2 * H * dk
    v_end = k_end + H * dv
    o_end = v_end + D_MODEL
    q = jnp.transpose(proj[..., :q_end].reshape(B, S, H, dk), (0, 2, 1, 3))
    k = jnp.transpose(proj[..., q_end:k_end].reshape(B, S, H, dk), (0, 2, 1, 3))
    v = jnp.transpose(proj[..., k_end:v_end].reshape(B, S, H, dv), (0, 2, 1, 3))
    o_gate = jax.nn.sigmoid(proj[..., v_end:o_end].astype(jnp.float32))
    gates = proj[..., o_end:].astype(jnp.float32) + b_gates.astype(jnp.float32)
    gates = GATE_SOFTCAP * jnp.tanh(gates / GATE_SOFTCAP)
    i_pre = jnp.transpose(gates[..., :H], (0, 2, 1))
    f_pre = jnp.transpose(gates[..., H:], (0, 2, 1))
    h = mlstm_chunkwise(q, k, v, i_pre, f_pre)
    h = h * lax.rsqrt(jnp.mean(h * h, axis=-1, keepdims=True) + RMS_EPS)
    h = jnp.transpose(h, (0, 2, 1, 3)).reshape(B, S, D_MODEL) * head_norm.astype(jnp.float32)
    y = (o_gate * h).astype(xn.dtype)
    return jnp.einsum('bsd,de->bse', y, w_out)


def swiglu(xn, w_gate_up, w_down):
    gu = jnp.einsum('bsd,df->bsf', xn, w_gate_up)
    g, u = gu[..., :D_FF], gu[..., D_FF:]
    return jnp.einsum('bsf,fd->bsd', jax.nn.silu(g) * u, w_down)


def setup_inputs(seed: int = 0) -> dict:
    key = jax.random.key(seed)
    ks = jax.random.split(key, 16)
    n_attn = (DEPTH + 1) // 2
    n_mlstm = DEPTH // 2
    out_scale = (2 * DEPTH) ** -0.5

    def dense(k, shape, fan_in, scale=1.0):
        return jax.random.normal(k, shape, jnp.float32) * (scale * fan_in ** -0.5)

    def gain(k, shape):
        return 1.0 + 0.02 * jax.random.normal(k, shape, jnp.float32)

    x = jax.random.normal(ks[0], (BATCH, SEQ, D_MODEL), jnp.float32)
    attn_norm = gain(ks[1], (n_attn, D_MODEL))
    attn_w_qkv = dense(ks[2], (n_attn, D_MODEL, 3 * D_MODEL), D_MODEL)
    attn_w_o = dense(ks[3], (n_attn, D_MODEL, D_MODEL), D_MODEL, out_scale)
    mlstm_norm = gain(ks[4], (n_mlstm, D_MODEL))
    mlstm_w_in = dense(ks[5], (n_mlstm, D_MODEL, MLSTM_IN_WIDTH), D_MODEL)
    i_bias = 0.1 * jax.random.normal(ks[6], (n_mlstm, MLSTM_HEADS), jnp.float32)
    f_bias = (jnp.linspace(3.0, 6.0, MLSTM_HEADS, dtype=jnp.float32)[None, :]
              + 0.1 * jax.random.normal(ks[7], (n_mlstm, MLSTM_HEADS), jnp.float32))
    mlstm_b_gates = jnp.concatenate([i_bias, f_bias], axis=-1)
    mlstm_head_norm = gain(ks[8], (n_mlstm, D_MODEL))
    mlstm_w_out = dense(ks[9], (n_mlstm, D_MODEL, D_MODEL), D_MODEL, out_scale)
    ffn_norm = gain(ks[10], (DEPTH, D_MODEL))
    ffn_w_gate_up = dense(ks[11], (DEPTH, D_MODEL, 2 * D_FF), D_MODEL)
    ffn_w_down = dense(ks[12], (DEPTH, D_FF, D_MODEL), D_FF, out_scale)
    final_norm = gain(ks[13], (D_MODEL,))
    return {"x": x, "attn_norm": attn_norm, "attn_w_qkv": attn_w_qkv, "attn_w_o": attn_w_o,
            "mlstm_norm": mlstm_norm, "mlstm_w_in": mlstm_w_in, "mlstm_b_gates": mlstm_b_gates,
            "mlstm_head_norm": mlstm_head_norm, "mlstm_w_out": mlstm_w_out,
            "ffn_norm": ffn_norm, "ffn_w_gate_up": ffn_w_gate_up, "ffn_w_down": ffn_w_down,
            "final_norm": final_norm}


def reference(x, attn_norm, attn_w_qkv, attn_w_o, mlstm_norm, mlstm_w_in, mlstm_b_gates,
              mlstm_head_norm, mlstm_w_out, ffn_norm, ffn_w_gate_up, ffn_w_down, final_norm):
    S = x.shape[1]
    cos, sin = rotary_tables(S)
    h = x
    for i in range(DEPTH):
        j = i // N_MIXERS
        if i % N_MIXERS == 0:
            h = h + moba_mixer(rmsnorm(h, attn_norm[j]), attn_w_qkv[j], attn_w_o[j], cos, sin)
        else:
            h = h + mlstm_mixer(rmsnorm(h, mlstm_norm[j]), mlstm_w_in[j], mlstm_b_gates[j],
                                mlstm_head_norm[j], mlstm_w_out[j])
        h = h + swiglu(rmsnorm(h, ffn_norm[i]), ffn_w_gate_up[i], ffn_w_down[i])
    return rmsnorm(h, final_norm)
```

```python
import functools

import jax
import jax.numpy as jnp
from jax import lax
from jax.experimental import pallas as pl
from jax.experimental.pallas import tpu as pltpu

F32 = jnp.float32
BF16 = jnp.bfloat16

D_MODEL = 1024
ATTN_HEADS = 16
ATTN_HEAD_DIM = 64
ROT_DIM = 16
ROT_HALF = ROT_DIM // 2
ROPE_THETA = 500000.0
MOBA_BLOCK = 256
MOBA_TOPK = 3
MLSTM_HEADS = 8
MLSTM_QK_DIM = 64
MLSTM_V_DIM = 128
GATE_SOFTCAP = 15.0
D_FF = 2816
RMS_EPS = 1e-6

LANES = 128
VMEM_LIMIT_BYTES = 56 * 1024 * 1024
PROJ_ROWS = 512
FFN_ROWS = 512
FFN_COLS = D_FF // 2
MLSTM_CHUNK = 128
MLSTM_ROWS = 512
NEG_INF = float("-inf")

_NT = (((1,), (1,)), ((), ()))


def _params(n_grid):
    return pltpu.CompilerParams(dimension_semantics=("arbitrary",) * n_grid,
                                vmem_limit_bytes=VMEM_LIMIT_BYTES)


def _rms(x, w):
    return x * lax.rsqrt(jnp.mean(x * x, axis=-1, keepdims=True) + RMS_EPS) * w


def _attn_proj_kernel(x_ref, nw_ref, wqt_ref, wk_ref, wvt_ref, cn_ref, sa_ref, sb_ref, cost_ref,
                      sint_ref, qt_ref, k_ref, vt_ref, kmean_ref):
    rows = x_ref.shape[1]
    xn = _rms(x_ref[0], nw_ref[...]).astype(BF16)

    k = jnp.dot(xn, wk_ref[...], preferred_element_type=F32)
    cn, sa, sb = cn_ref[...], sa_ref[...], sb_ref[...]
    for g in range(D_MODEL // LANES):
        kg = k[:, g * LANES:(g + 1) * LANES]
        kr = (kg * cn + pltpu.roll(kg, LANES - ROT_HALF, 1) * sa + pltpu.roll(kg, ROT_HALF, 1) * sb)
        k_ref[0, :, g * LANES:(g + 1) * LANES] = kr.astype(BF16)
        for t in range(rows // MOBA_BLOCK):
            kmean_ref[0, 0, t:t + 1, g * LANES:(g + 1) * LANES] = jnp.mean(
                kr[t * MOBA_BLOCK:(t + 1) * MOBA_BLOCK], axis=0, keepdims=True)

    qt = lax.dot_general(wqt_ref[...], xn, _NT, preferred_element_type=F32)
    ct, st = cost_ref[...], sint_ref[...]
    scale = ATTN_HEAD_DIM ** -0.5
    for h in range(ATTN_HEADS):
        blk = qt[h * ATTN_HEAD_DIM:(h + 1) * ATTN_HEAD_DIM]
        x1, x2 = blk[0:ROT_HALF], blk[ROT_HALF:ROT_DIM]
        r = jnp.concatenate([x1 * ct - x2 * st, x2 * ct + x1 * st, blk[ROT_DIM:]], axis=0) * scale
        qt_ref[0, h * ATTN_HEAD_DIM:(h + 1) * ATTN_HEAD_DIM, :] = r.astype(BF16)

    vt = lax.dot_general(wvt_ref[...], xn, _NT, preferred_element_type=F32)
    for t in range(rows // MOBA_BLOCK):
        vt_ref[0, t] = vt[:, t * MOBA_BLOCK:(t + 1) * MOBA_BLOCK].astype(BF16)


def _attn_proj(x, norm_w, wqt, wk, wvt, cn, sa, sb, cost, sint):
    B, S, D = x.shape
    rows = PROJ_ROWS
    nb_step = rows // MOBA_BLOCK
    const = lambda b, i: (0, 0)
    return pl.pallas_call(
        _attn_proj_kernel,
        grid=(B, S // rows),
        in_specs=[
            pl.BlockSpec((1, rows, D), lambda b, i: (b, i, 0)),
            pl.BlockSpec((1, D), const),
            pl.BlockSpec((D, D), const),
            pl.BlockSpec((D, D), const),
            pl.BlockSpec((D, D), const),
            pl.BlockSpec((rows, LANES), lambda b, i: (i, 0)),
            pl.BlockSpec((rows, LANES), lambda b, i: (i, 0)),
            pl.BlockSpec((rows, LANES), lambda b, i: (i, 0)),
            pl.BlockSpec((ROT_HALF, rows), lambda b, i: (0, i)),
            pl.BlockSpec((ROT_HALF, rows), lambda b, i: (0, i)),
        ],
        out_specs=[
            pl.BlockSpec((1, D, rows), lambda b, i: (b, 0, i)),
            pl.BlockSpec((1, rows, D), lambda b, i: (b, i, 0)),
            pl.BlockSpec((1, nb_step, D, MOBA_BLOCK), lambda b, i: (b, i, 0, 0)),
            pl.BlockSpec((1, 1, nb_step, D), lambda b, i: (b, i, 0, 0)),
        ],
        out_shape=[
            jax.ShapeDtypeStruct((B, D, S), BF16),
            jax.ShapeDtypeStruct((B, S, D), BF16),
            jax.ShapeDtypeStruct((B, S // MOBA_BLOCK, D, MOBA_BLOCK), BF16),
            jax.ShapeDtypeStruct((B, S // rows, nb_step, D), F32),
        ],
        compiler_params=_params(2),
        name="attn_proj",
    )(x, norm_w, wqt, wk, wvt, cn, sa, sb, cost, sint)


def _moba_kernel(qt_ref, k_ref, vt_ref, km_ref, o_ref, sel_ref):
    cur = pl.program_id(2)
    nblk = km_ref.shape[1]
    blk_q = qt_ref.shape[2]
    hd = ATTN_HEAD_DIM
    qt2 = qt_ref[0]
    row = lax.broadcasted_iota(jnp.int32, qt2.shape, 0)
    blk = lax.broadcasted_iota(jnp.int32, (nblk, blk_q), 0)
    kpos = lax.broadcasted_iota(jnp.int32, (MOBA_BLOCK, blk_q), 0)
    qpos = lax.broadcasted_iota(jnp.int32, (MOBA_BLOCK, blk_q), 1)
    km = km_ref[0].astype(BF16)
    ones_rows = jnp.ones((16, MOBA_BLOCK), BF16)

    qpads = []
    for a in range(2):
        qta = jnp.where((row >= a * hd) & (row < (a + 1) * hd), qt2, jnp.zeros_like(qt2))
        qpads.append(qta)
        gate = jnp.dot(km, qta, preferred_element_type=F32)
        gate = jnp.where(blk < cur, gate, NEG_INF)
        sel = jnp.zeros((nblk, blk_q), F32)
        for _ in range(MOBA_TOPK):
            mx = jnp.max(gate, axis=0, keepdims=True)
            idx = jnp.min(jnp.where(gate == mx, blk, nblk), axis=0, keepdims=True)
            hit = blk == idx
            sel = jnp.where(hit & (blk < cur), 1.0, sel)
            gate = jnp.where(hit, NEG_INF, gate)
        sel_ref[a] = sel

    def block_scores(j, a):
        k_j = k_ref[0, pl.ds(pl.multiple_of(j * MOBA_BLOCK, MOBA_BLOCK), MOBA_BLOCK), :]
        return jnp.dot(k_j, qpads[a], preferred_element_type=F32)

    def v_ext(j, a):
        return jnp.concatenate([vt_ref[0, j, a * hd:(a + 1) * hd, :], ones_rows], axis=0)

    carry = []
    for a in range(2):
        s = jnp.where(kpos <= qpos, block_scores(cur, a), NEG_INF)
        m = jnp.max(s, axis=0, keepdims=True)
        p = jnp.exp(s - m).astype(BF16)
        carry += [m, jnp.dot(v_ext(cur, a), p, preferred_element_type=F32)]

    def body(j, carry):
        out = []
        for a in range(2):
            m, acc = carry[2 * a], carry[2 * a + 1]
            s = block_scores(j, a)
            chosen = sel_ref[a, pl.ds(j, 1), :] > 0.5
            m_new = jnp.where(chosen, jnp.maximum(m, jnp.max(s, axis=0, keepdims=True)), m)
            p = jnp.exp(s - jnp.where(chosen, m_new, jnp.inf)).astype(BF16)
            acc = jnp.exp(m - m_new) * acc + jnp.dot(v_ext(j, a), p, preferred_element_type=F32)
            out += [m_new, acc]
        return tuple(out)

    carry = lax.fori_loop(0, cur, body, tuple(carry))
    outs = [carry[2 * a + 1][:hd] / carry[2 * a + 1][hd:hd + 1] for a in range(2)]
    o_ref[0] = jnp.concatenate(outs, axis=0).T.astype(BF16)


def _moba(qt, k, vt, kmean):
    B, D, S = qt.shape
    nblk = S // MOBA_BLOCK
    return pl.pallas_call(
        _moba_kernel,
        grid=(B, D // LANES, nblk),
        in_specs=[
            pl.BlockSpec((1, LANES, MOBA_BLOCK), lambda b, p, i: (b, p, i)),
            pl.BlockSpec((1, S, LANES), lambda b, p, i: (b, 0, p)),
            pl.BlockSpec((1, nblk, LANES, MOBA_BLOCK), lambda b, p, i: (b, 0, p, 0)),
            pl.BlockSpec((1, nblk, LANES), lambda b, p, i: (b, 0, p)),
        ],
        out_specs=pl.BlockSpec((1, MOBA_BLOCK, LANES), lambda b, p, i: (b, i, p)),
        out_shape=jax.ShapeDtypeStruct((B, S, D), BF16),
        scratch_shapes=[pltpu.VMEM((2, nblk, MOBA_BLOCK), F32)],
        compiler_params=_params(3),
        name="moba",
    )(qt, k, vt, kmean)


def _proj_ffn_kernel(a_ref, wp_ref, res_ref, nw_ref, wg_ref, wu_ref, wd_ref, fw_ref, o_ref,
                     h_sc, xn_sc, *, final_norm):
    f = pl.program_id(1)

    @pl.when(f == 0)
    def _():
        h = jnp.dot(a_ref[...], wp_ref[...], preferred_element_type=F32) + res_ref[...]
        h_sc[...] = h
        xn_sc[...] = _rms(h, nw_ref[...]).astype(BF16)

    xn = xn_sc[...]
    g = jnp.dot(xn, wg_ref[...], preferred_element_type=F32)
    u = jnp.dot(xn, wu_ref[...], preferred_element_type=F32)
    act = (g * jax.nn.sigmoid(g) * u).astype(BF16)
    h_sc[...] += jnp.dot(act, wd_ref[...], preferred_element_type=F32)

    @pl.when(f == pl.num_programs(1) - 1)
    def _():
        h = h_sc[...]
        o_ref[...] = _rms(h, fw_ref[...]) if final_norm else h


def _proj_ffn(a, wp, res, norm_w, wgu, wd, final_w, *, final_norm):
    T, D = res.shape
    rows, cols = FFN_ROWS, FFN_COLS
    nf = D_FF // cols
    const = lambda i, f: (0, 0)
    return pl.pallas_call(
        functools.partial(_proj_ffn_kernel, final_norm=final_norm),
        grid=(T // rows, nf),
        in_specs=[
            pl.BlockSpec((rows, D), lambda i, f: (i, 0)),
            pl.BlockSpec((D, D), const),
            pl.BlockSpec((rows, D), lambda i, f: (i, 0)),
            pl.BlockSpec((1, D), const),
            pl.BlockSpec((D, cols), lambda i, f: (0, f)),
            pl.BlockSpec((D, cols), lambda i, f: (0, nf + f)),
            pl.BlockSpec((cols, D), lambda i, f: (f, 0)),
            pl.BlockSpec((1, D), const),
        ],
        out_specs=pl.BlockSpec((rows, D), lambda i, f: (i, 0)),
        out_shape=jax.ShapeDtypeStruct((T, D), F32),
        scratch_shapes=[pltpu.VMEM((rows, D), F32), pltpu.VMEM((rows, D), BF16)],
        compiler_params=_params(2),
        name="proj_ffn_final" if final_norm else "proj_ffn",
    )(a, wp, res, norm_w, wgu, wgu, wd, final_w)


def _mlstm_proj_kernel(x_ref, nw_ref, wq_ref, wk_ref, wkt_ref, wv_ref, wo_ref, wgt_ref, bg_ref,
                       q_ref, k_ref, kt_ref, v_ref, og_ref, gt_ref):
    xn = _rms(x_ref[0], nw_ref[...]).astype(BF16)
    scale = MLSTM_QK_DIM ** -0.5
    q_ref[0] = (jnp.dot(xn, wq_ref[...], preferred_element_type=F32) * scale).astype(BF16)
    k_ref[0] = jnp.dot(xn, wk_ref[...], preferred_element_type=F32).astype(BF16)
    kt_ref[0] = lax.dot_general(wkt_ref[...], xn, _NT, preferred_element_type=F32).astype(BF16)
    v_ref[0] = jnp.dot(xn, wv_ref[...], preferred_element_type=F32).astype(BF16)
    og_ref[0] = jax.nn.sigmoid(jnp.dot(xn, wo_ref[...], preferred_element_type=F32)).astype(BF16)
    gates = lax.dot_general(wgt_ref[...], xn, _NT, preferred_element_type=F32) + bg_ref[...]
    gt_ref[0] = GATE_SOFTCAP * jnp.tanh(gates / GATE_SOFTCAP)


def _mlstm_proj(x, norm_w, wq, wk, wkt, wv, wo, wgt, bg):
    B, S, D = x.shape
    rows = PROJ_ROWS
    H, dk = MLSTM_HEADS, MLSTM_QK_DIM
    const = lambda b, i: (0, 0)
    tok = lambda w: pl.BlockSpec((1, rows, w), lambda b, i: (b, i, 0))
    return pl.pallas_call(
        _mlstm_proj_kernel,
        grid=(B, S // rows),
        in_specs=[
            tok(D),
            pl.BlockSpec((1, D), const),
            pl.BlockSpec((D, H * dk), const),
            pl.BlockSpec((D, H * dk), const),
            pl.BlockSpec((H * dk, D), const),
            pl.BlockSpec((D, D), const),
            pl.BlockSpec((D, D), const),
            pl.BlockSpec((2 * H, D), const),
            pl.BlockSpec((2 * H, 1), const),
        ],
        out_specs=[
            tok(H * dk), tok(H * dk),
            pl.BlockSpec((1, H * dk, rows), lambda b, i: (b, 0, i)),
            tok(D), tok(D),
            pl.BlockSpec((1, 2 * H, rows), lambda b, i: (b, 0, i)),
        ],
        out_shape=[
            jax.ShapeDtypeStruct((B, S, H * dk), BF16),
            jax.ShapeDtypeStruct((B, S, H * dk), BF16),
            jax.ShapeDtypeStruct((B, H * dk, S), BF16),
            jax.ShapeDtypeStruct((B, S, D), BF16),
            jax.ShapeDtypeStruct((B, S, D), BF16),
            jax.ShapeDtypeStruct((B, 2 * H, S), F32),
        ],
        compiler_params=_params(2),
        name="mlstm_proj",
    )(x, norm_w, wq, wk, wkt, wv, wo, wgt, bg)


def _mlstm_core_kernel(q_ref, k_ref, kt_ref, v_ref, og_ref, gt_ref, hw_ref, y_ref, c_sc, m_sc):
    L = MLSTM_CHUNK
    H, dk, dv = MLSTM_HEADS, MLSTM_QK_DIM, MLSTM_V_DIM

    @pl.when(pl.program_id(1) == 0)
    def _():
        c_sc[...] = jnp.zeros_like(c_sc)
        m_sc[...] = jnp.zeros_like(m_sc)

    lane8 = lax.broadcasted_iota(jnp.int32, (H, L), 1)
    l_idx = lax.broadcasted_iota(jnp.int32, (L, L), 0)
    s_idx = lax.broadcasted_iota(jnp.int32, (L, L), 1)
    causal = s_idx <= l_idx
    lane = lax.broadcasted_iota(jnp.int32, (L, LANES), 1)
    one_col = jnp.where(lane == 0, 1.0, 0.0).astype(BF16)
    hw = hw_ref[...]

    for c in range(q_ref.shape[1] // L):
        rows = slice(c * L, (c + 1) * L)
        gates = gt_ref[0, :, rows]
        ipre, fpre = gates[0:H], gates[H:2 * H]
        logf = jnp.minimum(fpre, 0.0) - jnp.log1p(jnp.exp(-jnp.abs(fpre)))
        b = logf
        sh = 1
        while sh < L:
            b = b + jnp.where(lane8 >= sh, pltpu.roll(b, sh, 1), 0.0)
            sh *= 2
        u = ipre - b

        for hp in range(H // 2):
            q2 = q_ref[0, rows, hp * LANES:(hp + 1) * LANES]
            k2 = k_ref[0, rows, hp * LANES:(hp + 1) * LANES]
            for a in range(2):
                h = 2 * hp + a
                qa = jnp.where((lane >= a * dk) & (lane < (a + 1) * dk), q2, jnp.zeros_like(q2))
                s = lax.dot_general(qa, k2, _NT, preferred_element_type=F32)
                uh, lfh = u[h:h + 1], logf[h:h + 1]
                m_prev = m_sc[h:h + 1, 0:1]
                um = jnp.where(causal, uh, NEG_INF)
                big_m = jnp.maximum(jnp.max(um, axis=1, keepdims=True), m_prev)
                e = jnp.exp(um - big_m)
                b_col = jnp.sum(jnp.where(causal, lfh, 0.0), axis=1, keepdims=True)
                inter = jnp.exp(m_prev - big_m)
                floor = jnp.exp(-(b_col + big_m))
                v_ext = jnp.concatenate([v_ref[0, rows, h * dv:(h + 1) * dv], one_col], axis=1)
                c_pair = c_sc[hp]
                r = (inter * jnp.dot(qa, c_pair.astype(BF16), preferred_element_type=F32)
                     + jnp.dot((s * e).astype(BF16), v_ext, preferred_element_type=F32))
                hh = r[:, :dv] / jnp.maximum(jnp.abs(r[:, dv:dv + 1]), floor)
                hn = _rms(hh, hw[:, h * dv:(h + 1) * dv])
                y_ref[0, rows, h * dv:(h + 1) * dv] = (
                    og_ref[0, rows, h * dv:(h + 1) * dv].astype(F32) * hn).astype(BF16)

                m_last = jnp.maximum(jnp.max(uh, axis=1, keepdims=True), m_prev)
                w = jnp.exp(uh - m_last)
                kw = (kt_ref[0, h * dk:(h + 1) * dk, rows].astype(F32) * w).astype(BF16)
                c_sc[hp, a * dk:(a + 1) * dk, :] = (
                    jnp.exp(m_prev - m_last) * c_pair[a * dk:(a + 1) * dk]
                    + jnp.dot(kw, v_ext, preferred_element_type=F32))
                m_sc[h:h + 1, :] = jnp.broadcast_to(b[h:h + 1, L - 1:L] + m_last, (1, LANES))


def _mlstm_core(q, k, kt, v, og, gt, head_w):
    B, S, D = v.shape
    rows = MLSTM_ROWS
    H, dk, dv = MLSTM_HEADS, MLSTM_QK_DIM, MLSTM_V_DIM
    tok = lambda w: pl.BlockSpec((1, rows, w), lambda b, i: (b, i, 0))
    return pl.pallas_call(
        _mlstm_core_kernel,
        grid=(B, S // rows),
        in_specs=[
            tok(H * dk), tok(H * dk),
            pl.BlockSpec((1, H * dk, rows), lambda b, i: (b, 0, i)),
            tok(D), tok(D),
            pl.BlockSpec((1, 2 * H, rows), lambda b, i: (b, 0, i)),
            pl.BlockSpec((1, D), lambda b, i: (0, 0)),
        ],
        out_specs=tok(D),
        out_shape=jax.ShapeDtypeStruct((B, S, D), BF16),
        scratch_shapes=[pltpu.VMEM((H // 2, 2 * dk, 2 * dv), F32), pltpu.VMEM((H, LANES), F32)],
        compiler_params=_params(2),
        name="mlstm_core",
    )(q, k, kt, v, og, gt, head_w)


def _rotary_tables(seq):
    pos = jnp.arange(seq, dtype=F32)
    inv_freq = ROPE_THETA ** (-jnp.arange(0, ROT_DIM, 2, dtype=F32) / ROT_DIM)
    ang = pos[:, None] * inv_freq[None, :]
    cos, sin = jnp.cos(ang), jnp.sin(ang)
    d = jnp.arange(LANES) % ATTN_HEAD_DIM
    cos_l, sin_l = cos[:, d % ROT_HALF], sin[:, d % ROT_HALF]
    cn = jnp.where(d < ROT_DIM, cos_l, 1.0)
    sa = jnp.where(d < ROT_HALF, -sin_l, 0.0)
    sb = jnp.where((d >= ROT_HALF) & (d < ROT_DIM), sin_l, 0.0)
    return cn, sa, sb, cos.T, sin.T


def kernel(x, attn_norm, attn_w_qkv, attn_w_o, mlstm_norm, mlstm_w_in, mlstm_b_gates, mlstm_head_norm,
           mlstm_w_out, ffn_norm, ffn_w_gate_up, ffn_w_down, final_norm):
    B, S, D = x.shape
    T = B * S
    H, dk = MLSTM_HEADS, MLSTM_QK_DIM
    row = lambda w: w.reshape(1, -1).astype(F32)

    cn, sa, sb, cost, sint = _rotary_tables(S)
    wqkv = attn_w_qkv[0].astype(BF16)
    qt, k, vt, kmean = _attn_proj(x, row(attn_norm[0]), wqkv[:, :D].T, wqkv[:, D:2 * D], wqkv[:, 2 * D:].T,
                                  cn, sa, sb, cost, sint)
    o = _moba(qt, k, vt, kmean.reshape(B, S // MOBA_BLOCK, D))
    h = _proj_ffn(o.reshape(T, D), attn_w_o[0].astype(BF16), x.reshape(T, D), row(ffn_norm[0]),
                  ffn_w_gate_up[0].astype(BF16), ffn_w_down[0].astype(BF16), row(final_norm),
                  final_norm=False)

    w_in = mlstm_w_in[0].astype(BF16)
    q_end, k_end, v_end, o_end = H * dk, 2 * H * dk, 2 * H * dk + D, 2 * H * dk + 2 * D
    q, k, kt, v, og, gt = _mlstm_proj(
        h.reshape(B, S, D), row(mlstm_norm[0]), w_in[:, :q_end], w_in[:, q_end:k_end],
        w_in[:, q_end:k_end].T, w_in[:, k_end:v_end], w_in[:, v_end:o_end], w_in[:, o_end:].T,
        mlstm_b_gates[0].reshape(2 * H, 1).astype(F32))
    y = _mlstm_core(q, k, kt, v, og, gt, row(mlstm_head_norm[0]))
    out = _proj_ffn(y.reshape(T, D), mlstm_w_out[0].astype(BF16), h, row(ffn_norm[1]),
                    ffn_w_gate_up[1].astype(BF16), ffn_w_down[1].astype(BF16), row(final_norm),
                    final_norm=True)
    return out.reshape(B, S, D)
```

```python
import functools

import jax
import jax.numpy as jnp
from jax import lax
from jax.experimental import pallas as pl
from jax.experimental.pallas import tpu as pltpu

F32 = jnp.float32
BF16 = jnp.bfloat16

D_MODEL = 1024
ATTN_HEADS = 16
ATTN_HEAD_DIM = 64
ROT_DIM = 16
ROT_HALF = ROT_DIM // 2
ROPE_THETA = 500000.0
MOBA_BLOCK = 256
MOBA_TOPK = 3
MLSTM_HEADS = 8
MLSTM_QK_DIM = 64
MLSTM_V_DIM = 128
GATE_SOFTCAP = 15.0
D_FF = 2816
RMS_EPS = 1e-6

LANES = 128
VMEM_LIMIT_BYTES = 56 * 1024 * 1024
PROJ_ROWS = 512
FFN_ROWS = 512
FFN_COLS = D_FF // 2
MLSTM_CHUNK = 128
MLSTM_ROWS = 512
NEG_INF = float("-inf")

_NT = (((1,), (1,)), ((), ()))


def _params(n_grid):
    return pltpu.CompilerParams(dimension_semantics=("arbitrary",) * n_grid,
                                vmem_limit_bytes=VMEM_LIMIT_BYTES)


def _rms(x, w):
    return x * lax.rsqrt(jnp.mean(x * x, axis=-1, keepdims=True) + RMS_EPS) * w


def _attn_proj_kernel(x_ref, nw_ref, wqt_ref, wk_ref, wvt_ref, cn_ref, sa_ref, sb_ref, cost_ref,
                      sint_ref, qt_ref, k_ref, vt_ref, kmean_ref):
    rows = x_ref.shape[1]
    xn = _rms(x_ref[0], nw_ref[...]).astype(BF16)

    k = jnp.dot(xn, wk_ref[...], preferred_element_type=F32)
    cn, sa, sb = cn_ref[...], sa_ref[...], sb_ref[...]
    for g in range(D_MODEL // LANES):
        kg = k[:, g * LANES:(g + 1) * LANES]
        kr = (kg * cn + pltpu.roll(kg, LANES - ROT_HALF, 1) * sa + pltpu.roll(kg, ROT_HALF, 1) * sb)
        k_ref[0, :, g * LANES:(g + 1) * LANES] = kr.astype(BF16)
        for t in range(rows // MOBA_BLOCK):
            kmean_ref[0, 0, t:t + 1, g * LANES:(g + 1) * LANES] = jnp.mean(
                kr[t * MOBA_BLOCK:(t + 1) * MOBA_BLOCK], axis=0, keepdims=True)

    qt = lax.dot_general(wqt_ref[...], xn, _NT, preferred_element_type=F32)
    ct, st = cost_ref[...], sint_ref[...]
    scale = ATTN_HEAD_DIM ** -0.5
    for h in range(ATTN_HEADS):
        blk = qt[h * ATTN_HEAD_DIM:(h + 1) * ATTN_HEAD_DIM]
        x1, x2 = blk[0:ROT_HALF], blk[ROT_HALF:ROT_DIM]
        r = jnp.concatenate([x1 * ct - x2 * st, x2 * ct + x1 * st, blk[ROT_DIM:]], axis=0) * scale
        qt_ref[0, h * ATTN_HEAD_DIM:(h + 1) * ATTN_HEAD_DIM, :] = r.astype(BF16)

    vt = lax.dot_general(wvt_ref[...], xn, _NT, preferred_element_type=F32)
    for t in range(rows // MOBA_BLOCK):
        vt_ref[0, t] = vt[:, t * MOBA_BLOCK:(t + 1) * MOBA_BLOCK].astype(BF16)


def _attn_proj(x, norm_w, wqt, wk, wvt, cn, sa, sb, cost, sint):
    B, S, D = x.shape
    rows = PROJ_ROWS
    nb_step = rows // MOBA_BLOCK
    const = lambda b, i: (0, 0)
    return pl.pallas_call(
        _attn_proj_kernel,
        grid=(B, S // rows),
        in_specs=[
            pl.BlockSpec((1, rows, D), lambda b, i: (b, i, 0)),
            pl.BlockSpec((1, D), const),
            pl.BlockSpec((D, D), const),
            pl.BlockSpec((D, D), const),
            pl.BlockSpec((D, D), const),
            pl.BlockSpec((rows, LANES), lambda b, i: (i, 0)),
            pl.BlockSpec((rows, LANES), lambda b, i: (i, 0)),
            pl.BlockSpec((rows, LANES), lambda b, i: (i, 0)),
            pl.BlockSpec((ROT_HALF, rows), lambda b, i: (0, i)),
            pl.BlockSpec((ROT_HALF, rows), lambda b, i: (0, i)),
        ],
        out_specs=[
            pl.BlockSpec((1, D, rows), lambda b, i: (b, 0, i)),
            pl.BlockSpec((1, rows, D), lambda b, i: (b, i, 0)),
            pl.BlockSpec((1, nb_step, D, MOBA_BLOCK), lambda b, i: (b, i, 0, 0)),
            pl.BlockSpec((1, 1, nb_step, D), lambda b, i: (b, i, 0, 0)),
        ],
        out_shape=[
            jax.ShapeDtypeStruct((B, D, S), BF16),
            jax.ShapeDtypeStruct((B, S, D), BF16),
            jax.ShapeDtypeStruct((B, S // MOBA_BLOCK, D, MOBA_BLOCK), BF16),
            jax.ShapeDtypeStruct((B, S // rows, nb_step, D), F32),
        ],
        compiler_params=_params(2),
        name="attn_proj",
    )(x, norm_w, wqt, wk, wvt, cn, sa, sb, cost, sint)


def _moba_kernel(qt_ref, k_ref, vt_ref, km_ref, o_ref, sel_ref, s_sc):
    cur = pl.program_id(2)
    nblk = km_ref.shape[1]
    blk_q = qt_ref.shape[2]
    hd = ATTN_HEAD_DIM
    qt2 = qt_ref[0]
    row = lax.broadcasted_iota(jnp.int32, qt2.shape, 0)
    blk = lax.broadcasted_iota(jnp.int32, (nblk, blk_q), 0)
    kpos = lax.broadcasted_iota(jnp.int32, (MOBA_BLOCK, blk_q), 0)
    qpos = lax.broadcasted_iota(jnp.int32, (MOBA_BLOCK, blk_q), 1)
    km = km_ref[0].astype(BF16)
    ones_rows = jnp.ones((16, MOBA_BLOCK), BF16)

    qpads = []
    for a in range(2):
        qta = jnp.where((row >= a * hd) & (row < (a + 1) * hd), qt2, jnp.zeros_like(qt2))
        qpads.append(qta)
        gate = jnp.dot(km, qta, preferred_element_type=F32)
        gate = jnp.where(blk < cur, gate, NEG_INF)
        sel = jnp.zeros((nblk, blk_q), F32)
        for _ in range(MOBA_TOPK):
            mx = jnp.max(gate, axis=0, keepdims=True)
            idx = jnp.min(jnp.where(gate == mx, blk, nblk), axis=0, keepdims=True)
            hit = blk == idx
            sel = jnp.where(hit & (blk < cur), 1.0, sel)
            gate = jnp.where(hit, NEG_INF, gate)
        sel_ref[a] = sel

    def block_scores(j, a):
        k_j = k_ref[0, pl.ds(pl.multiple_of(j * MOBA_BLOCK, MOBA_BLOCK), MOBA_BLOCK), :]
        return jnp.dot(k_j, qpads[a], preferred_element_type=F32)

    def v_ext(j, a):
        return jnp.concatenate([vt_ref[0, j, a * hd:(a + 1) * hd, :], ones_rows], axis=0)

    carry = []
    for a in range(2):
        s = jnp.where(kpos <= qpos, block_scores(cur, a), NEG_INF)
        m = jnp.max(s, axis=0, keepdims=True)
        p = jnp.exp(s - m).astype(BF16)
        carry += [m, jnp.dot(v_ext(cur, a), p, preferred_element_type=F32)]

    def scores(j):
        jj = jnp.minimum(j, nblk - 1)
        return [block_scores(jj, a) for a in range(2)]

    def park(s, slot):
        for a in range(2):
            s_sc[slot, a] = s[a]
        return [jnp.max(s[a], axis=0, keepdims=True) for a in range(2)]

    def consume(j, slot, smax, m_acc):
        jj = jnp.minimum(j, nblk - 1)
        out = []
        for a in range(2):
            m, acc = m_acc[2 * a], m_acc[2 * a + 1]
            chosen = sel_ref[a, pl.ds(jj, 1), :] > 0.5
            m_new = jnp.where(chosen, jnp.maximum(m, smax[a]), m)
            p = jnp.exp(s_sc[slot, a] - jnp.where(chosen, m_new, jnp.inf)).astype(BF16)
            acc = jnp.exp(m - m_new) * acc + jnp.dot(v_ext(jj, a), p, preferred_element_type=F32)
            out += [m_new, acc]
        return out

    def body(i, carry):
        m_acc, smax0, smax1 = list(carry[:4]), list(carry[4:6]), list(carry[6:])
        s_next = scores(2 * i + 2)
        m_acc = consume(2 * i, 0, smax0, m_acc)
        smax0 = park(s_next, 0)
        s_next = scores(2 * i + 3)
        m_acc = consume(2 * i + 1, 1, smax1, m_acc)
        smax1 = park(s_next, 1)
        return tuple(m_acc + smax0 + smax1)

    init = tuple(carry + park(scores(0), 0) + park(scores(1), 1))
    carry = lax.fori_loop(0, (cur + 1) // 2, body, init)
    outs = [carry[2 * a + 1][:hd] / carry[2 * a + 1][hd:hd + 1] for a in range(2)]
    o_ref[0] = jnp.concatenate(outs, axis=0).T.astype(BF16)


def _moba(qt, k, vt, kmean):
    B, D, S = qt.shape
    nblk = S // MOBA_BLOCK
    return pl.pallas_call(
        _moba_kernel,
        grid=(B, D // LANES, nblk),
        in_specs=[
            pl.BlockSpec((1, LANES, MOBA_BLOCK), lambda b, p, i: (b, p, i)),
            pl.BlockSpec((1, S, LANES), lambda b, p, i: (b, 0, p)),
            pl.BlockSpec((1, nblk, LANES, MOBA_BLOCK), lambda b, p, i: (b, 0, p, 0)),
            pl.BlockSpec((1, nblk, LANES), lambda b, p, i: (b, 0, p)),
        ],
        out_specs=pl.BlockSpec((1, MOBA_BLOCK, LANES), lambda b, p, i: (b, i, p)),
        out_shape=jax.ShapeDtypeStruct((B, S, D), BF16),
        scratch_shapes=[pltpu.VMEM((2, nblk, MOBA_BLOCK), F32),
                        pltpu.VMEM((2, 2, MOBA_BLOCK, MOBA_BLOCK), F32)],
        compiler_params=_params(3),
        name="moba",
    )(qt, k, vt, kmean)


def _proj_ffn_kernel(a_ref, wp_ref, res_ref, nw_ref, wg_ref, wu_ref, wd_ref, fw_ref, o_ref,
                     h_sc, xn_sc, *, final_norm):
    f = pl.program_id(1)

    @pl.when(f == 0)
    def _():
        h = jnp.dot(a_ref[...], wp_ref[...], preferred_element_type=F32) + res_ref[...]
        h_sc[...] = h
        xn_sc[...] = _rms(h, nw_ref[...]).astype(BF16)

    xn = xn_sc[...]
    g = jnp.dot(xn, wg_ref[...], preferred_element_type=F32)
    u = jnp.dot(xn, wu_ref[...], preferred_element_type=F32)
    act = (g * jax.nn.sigmoid(g) * u).astype(BF16)
    h_sc[...] += jnp.dot(act, wd_ref[...], preferred_element_type=F32)

    @pl.when(f == pl.num_programs(1) - 1)
    def _():
        h = h_sc[...]
        o_ref[...] = _rms(h, fw_ref[...]) if final_norm else h


def _proj_ffn(a, wp, res, norm_w, wgu, wd, final_w, *, final_norm):
    T, D = res.shape
    rows, cols = FFN_ROWS, FFN_COLS
    nf = D_FF // cols
    const = lambda i, f: (0, 0)
    return pl.pallas_call(
        functools.partial(_proj_ffn_kernel, final_norm=final_norm),
        grid=(T // rows, nf),
        in_specs=[
            pl.BlockSpec((rows, D), lambda i, f: (i, 0)),
            pl.BlockSpec((D, D), const),
            pl.BlockSpec((rows, D), lambda i, f: (i, 0)),
            pl.BlockSpec((1, D), const),
            pl.BlockSpec((D, cols), lambda i, f: (0, f)),
            pl.BlockSpec((D, cols), lambda i, f: (0, nf + f)),
            pl.BlockSpec((cols, D), lambda i, f: (f, 0)),
            pl.BlockSpec((1, D), const),
        ],
        out_specs=pl.BlockSpec((rows, D), lambda i, f: (i, 0)),
        out_shape=jax.ShapeDtypeStruct((T, D), F32),
        scratch_shapes=[pltpu.VMEM((rows, D), F32), pltpu.VMEM((rows, D), BF16)],
        compiler_params=_params(2),
        name="proj_ffn_final" if final_norm else "proj_ffn",
    )(a, wp, res, norm_w, wgu, wgu, wd, final_w)


def _mlstm_proj_kernel(x_ref, nw_ref, wq_ref, wk_ref, wkt_ref, wv_ref, wo_ref, wgt_ref, bg_ref,
                       q_ref, k_ref, kt_ref, v_ref, og_ref, gt_ref):
    xn = _rms(x_ref[0], nw_ref[...]).astype(BF16)
    scale = MLSTM_QK_DIM ** -0.5
    q_ref[0] = (jnp.dot(xn, wq_ref[...], preferred_element_type=F32) * scale).astype(BF16)
    k_ref[0] = jnp.dot(xn, wk_ref[...], preferred_element_type=F32).astype(BF16)
    kt_ref[0] = lax.dot_general(wkt_ref[...], xn, _NT, preferred_element_type=F32).astype(BF16)
    v_ref[0] = jnp.dot(xn, wv_ref[...], preferred_element_type=F32).astype(BF16)
    og_ref[0] = jax.nn.sigmoid(jnp.dot(xn, wo_ref[...], preferred_element_type=F32)).astype(BF16)
    gates = lax.dot_general(wgt_ref[...], xn, _NT, preferred_element_type=F32) + bg_ref[...]
    gt_ref[0] = GATE_SOFTCAP * jnp.tanh(gates / GATE_SOFTCAP)


def _mlstm_proj(x, norm_w, wq, wk, wkt, wv, wo, wgt, bg):
    B, S, D = x.shape
    rows = PROJ_ROWS
    H, dk = MLSTM_HEADS, MLSTM_QK_DIM
    const = lambda b, i: (0, 0)
    tok = lambda w: pl.BlockSpec((1, rows, w), lambda b, i: (b, i, 0))
    return pl.pallas_call(
        _mlstm_proj_kernel,
        grid=(B, S // rows),
        in_specs=[
            tok(D),
            pl.BlockSpec((1, D), const),
            pl.BlockSpec((D, H * dk), const),
            pl.BlockSpec((D, H * dk), const),
            pl.BlockSpec((H * dk, D), const),
            pl.BlockSpec((D, D), const),
            pl.BlockSpec((D, D), const),
            pl.BlockSpec((2 * H, D), const),
            pl.BlockSpec((2 * H, 1), const),
        ],
        out_specs=[
            tok(H * dk), tok(H * dk),
            pl.BlockSpec((1, H * dk, rows), lambda b, i: (b, 0, i)),
            tok(D), tok(D),
            pl.BlockSpec((1, 2 * H, rows), lambda b, i: (b, 0, i)),
        ],
        out_shape=[
            jax.ShapeDtypeStruct((B, S, H * dk), BF16),
            jax.ShapeDtypeStruct((B, S, H * dk), BF16),
            jax.ShapeDtypeStruct((B, H * dk, S), BF16),
            jax.ShapeDtypeStruct((B, S, D), BF16),
            jax.ShapeDtypeStruct((B, S, D), BF16),
            jax.ShapeDtypeStruct((B, 2 * H, S), F32),
        ],
        compiler_params=_params(2),
        name="mlstm_proj",
    )(x, norm_w, wq, wk, wkt, wv, wo, wgt, bg)


def _mlstm_core_kernel(q_ref, k_ref, kt_ref, v_ref, og_ref, gt_ref, hw_ref, y_ref, c_sc, m_sc):
    L = MLSTM_CHUNK
    H, dk, dv = MLSTM_HEADS, MLSTM_QK_DIM, MLSTM_V_DIM

    @pl.when(pl.program_id(1) == 0)
    def _():
        c_sc[...] = jnp.zeros_like(c_sc)
        m_sc[...] = jnp.zeros_like(m_sc)

    lane8 = lax.broadcasted_iota(jnp.int32, (H, L), 1)
    l_idx = lax.broadcasted_iota(jnp.int32, (L, L), 0)
    s_idx = lax.broadcasted_iota(jnp.int32, (L, L), 1)
    causal = s_idx <= l_idx
    lane = lax.broadcasted_iota(jnp.int32, (L, LANES), 1)
    one_col = jnp.where(lane == 0, 1.0, 0.0).astype(BF16)
    hw = hw_ref[...]

    for c in range(q_ref.shape[1] // L):
        rows = slice(c * L, (c + 1) * L)
        gates = gt_ref[0, :, rows]
        ipre, fpre = gates[0:H], gates[H:2 * H]
        logf = jnp.minimum(fpre, 0.0) - jnp.log1p(jnp.exp(-jnp.abs(fpre)))
        b = logf
        sh = 1
        while sh < L:
            b = b + jnp.where(lane8 >= sh, pltpu.roll(b, sh, 1), 0.0)
            sh *= 2
        u = ipre - b

        for hp in range(H // 2):
            q2 = q_ref[0, rows, hp * LANES:(hp + 1) * LANES]
            k2 = k_ref[0, rows, hp * LANES:(hp + 1) * LANES]
            for a in range(2):
                h = 2 * hp + a
                qa = jnp.where((lane >= a * dk) & (lane < (a + 1) * dk), q2, jnp.zeros_like(q2))
                s = lax.dot_general(qa, k2, _NT, preferred_element_type=F32)
                uh, lfh = u[h:h + 1], logf[h:h + 1]
                m_prev = m_sc[h:h + 1, 0:1]
                um = jnp.where(causal, uh, NEG_INF)
                big_m = jnp.maximum(jnp.max(um, axis=1, keepdims=True), m_prev)
                e = jnp.exp(um - big_m)
                b_col = jnp.sum(jnp.where(causal, lfh, 0.0), axis=1, keepdims=True)
                inter = jnp.exp(m_prev - big_m)
                floor = jnp.exp(-(b_col + big_m))
                v_ext = jnp.concatenate([v_ref[0, rows, h * dv:(h + 1) * dv], one_col], axis=1)
                c_pair = c_sc[hp]
                r = (inter * jnp.dot(qa, c_pair.astype(BF16), preferred_element_type=F32)
                     + jnp.dot((s * e).astype(BF16), v_ext, preferred_element_type=F32))
                hh = r[:, :dv] / jnp.maximum(jnp.abs(r[:, dv:dv + 1]), floor)
                hn = _rms(hh, hw[:, h * dv:(h + 1) * dv])
                y_ref[0, rows, h * dv:(h + 1) * dv] = (
                    og_ref[0, rows, h * dv:(h + 1) * dv].astype(F32) * hn).astype(BF16)

                m_last = jnp.maximum(jnp.max(uh, axis=1, keepdims=True), m_prev)
                w = jnp.exp(uh - m_last)
                kw = (kt_ref[0, h * dk:(h + 1) * dk, rows].astype(F32) * w).astype(BF16)
                c_sc[hp, a * dk:(a + 1) * dk, :] = (
                    jnp.exp(m_prev - m_last) * c_pair[a * dk:(a + 1) * dk]
                    + jnp.dot(kw, v_ext, preferred_element_type=F32))
                m_sc[h:h + 1, :] = jnp.broadcast_to(b[h:h + 1, L - 1:L] + m_last, (1, LANES))


def _mlstm_core(q, k, kt, v, og, gt, head_w):
    B, S, D = v.shape
    rows = MLSTM_ROWS
    H, dk, dv = MLSTM_HEADS, MLSTM_QK_DIM, MLSTM_V_DIM
    tok = lambda w: pl.BlockSpec((1, rows, w), lambda b, i: (b, i, 0))
    return pl.pallas_call(
        _mlstm_core_kernel,
        grid=(B, S // rows),
        in_specs=[
            tok(H * dk), tok(H * dk),
            pl.BlockSpec((1, H * dk, rows), lambda b, i: (b, 0, i)),
            tok(D), tok(D),
            pl.BlockSpec((1, 2 * H, rows), lambda b, i: (b, 0, i)),
            pl.BlockSpec((1, D), lambda b, i: (0, 0)),
        ],
        out_specs=tok(D),
        out_shape=jax.ShapeDtypeStruct((B, S, D), BF16),
        scratch_shapes=[pltpu.VMEM((H // 2, 2 * dk, 2 * dv), F32), pltpu.VMEM((H, LANES), F32)],
        compiler_params=_params(2),
        name="mlstm_core",
    )(q, k, kt, v, og, gt, head_w)


def _rotary_tables(seq):
    pos = jnp.arange(seq, dtype=F32)
    inv_freq = ROPE_THETA ** (-jnp.arange(0, ROT_DIM, 2, dtype=F32) / ROT_DIM)
    ang = pos[:, None] * inv_freq[None, :]
    cos, sin = jnp.cos(ang), jnp.sin(ang)
    d = jnp.arange(LANES) % ATTN_HEAD_DIM
    cos_l, sin_l = cos[:, d % ROT_HALF], sin[:, d % ROT_HALF]
    cn = jnp.where(d < ROT_DIM, cos_l, 1.0)
    sa = jnp.where(d < ROT_HALF, -sin_l, 0.0)
    sb = jnp.where((d >= ROT_HALF) & (d < ROT_DIM), sin_l, 0.0)
    return cn, sa, sb, cos.T, sin.T


def kernel(x, attn_norm, attn_w_qkv, attn_w_o, mlstm_norm, mlstm_w_in, mlstm_b_gates, mlstm_head_norm,
           mlstm_w_out, ffn_norm, ffn_w_gate_up, ffn_w_down, final_norm):
    B, S, D = x.shape
    T = B * S
    H, dk = MLSTM_HEADS, MLSTM_QK_DIM
    row = lambda w: w.reshape(1, -1).astype(F32)

    cn, sa, sb, cost, sint = _rotary_tables(S)
    wqkv = attn_w_qkv[0].astype(BF16)
    qt, k, vt, kmean = _attn_proj(x, row(attn_norm[0]), wqkv[:, :D].T, wqkv[:, D:2 * D], wqkv[:, 2 * D:].T,
                                  cn, sa, sb, cost, sint)
    o = _moba(qt, k, vt, kmean.reshape(B, S // MOBA_BLOCK, D))
    h = _proj_ffn(o.reshape(T, D), attn_w_o[0].astype(BF16), x.reshape(T, D), row(ffn_norm[0]),
                  ffn_w_gate_up[0].astype(BF16), ffn_w_down[0].astype(BF16), row(final_norm),
                  final_norm=False)

    w_in = mlstm_w_in[0].astype(BF16)
    q_end, k_end, v_end, o_end = H * dk, 2 * H * dk, 2 * H * dk + D, 2 * H * dk + 2 * D
    q, k, kt, v, og, gt = _mlstm_proj(
        h.reshape(B, S, D), row(mlstm_norm[0]), w_in[:, :q_end], w_in[:, q_end:k_end],
        w_in[:, q_end:k_end].T, w_in[:, k_end:v_end], w_in[:, v_end:o_end], w_in[:, o_end:].T,
        mlstm_b_gates[0].reshape(2 * H, 1).astype(F32))
    y = _mlstm_core(q, k, kt, v, og, gt, row(mlstm_head_norm[0]))
    out = _proj_ffn(y.reshape(T, D), mlstm_w_out[0].astype(BF16), h, row(ffn_norm[1]),
                    ffn_w_gate_up[1].astype(BF16), ffn_w_down[1].astype(BF16), row(final_norm),
                    final_norm=True)
    return out.reshape(B, S, D)
```

```python
import functools

import jax
import jax.numpy as jnp
from jax import lax
from jax.experimental import pallas as pl
from jax.experimental.pallas import tpu as pltpu

F32 = jnp.float32
BF16 = jnp.bfloat16

D_MODEL = 1024
ATTN_HEADS = 16
ATTN_HEAD_DIM = 64
ROT_DIM = 16
ROT_HALF = ROT_DIM // 2
ROPE_THETA = 500000.0
MOBA_BLOCK = 256
MOBA_TOPK = 3
MLSTM_HEADS = 8
MLSTM_QK_DIM = 64
MLSTM_V_DIM = 128
GATE_SOFTCAP = 15.0
D_FF = 2816
RMS_EPS = 1e-6

LANES = 128
VMEM_LIMIT_BYTES = 56 * 1024 * 1024
PROJ_ROWS = 512
FFN_ROWS = 512
FFN_COLS = D_FF // 2
MLSTM_CHUNK = 128
MLSTM_ROWS = 512
MOBA_HEADS = 4
LOG2E = 1.4426950408889634
NEG_INF = float("-inf")

_NT = (((1,), (1,)), ((), ()))


def _params(n_grid):
    return pltpu.CompilerParams(dimension_semantics=("arbitrary",) * n_grid,
                                vmem_limit_bytes=VMEM_LIMIT_BYTES)


def _rms(x, w):
    return x * lax.rsqrt(jnp.mean(x * x, axis=-1, keepdims=True) + RMS_EPS) * w


def _attn_proj_kernel(x_ref, nw_ref, wqt_ref, wk_ref, wvt_ref, cn_ref, sa_ref, sb_ref, cost_ref,
                      sint_ref, qt_ref, k_ref, vt_ref, kmean_ref):
    rows = x_ref.shape[1]
    xn = _rms(x_ref[0], nw_ref[...]).astype(BF16)

    k = jnp.dot(xn, wk_ref[...], preferred_element_type=F32)
    cn, sa, sb = cn_ref[...], sa_ref[...], sb_ref[...]
    for g in range(D_MODEL // LANES):
        kg = k[:, g * LANES:(g + 1) * LANES]
        kr = (kg * cn + pltpu.roll(kg, LANES - ROT_HALF, 1) * sa + pltpu.roll(kg, ROT_HALF, 1) * sb)
        k_ref[0, :, g * LANES:(g + 1) * LANES] = kr.astype(BF16)
        for t in range(rows // MOBA_BLOCK):
            kmean_ref[0, 0, t:t + 1, g * LANES:(g + 1) * LANES] = jnp.mean(
                kr[t * MOBA_BLOCK:(t + 1) * MOBA_BLOCK], axis=0, keepdims=True)

    qt = lax.dot_general(wqt_ref[...], xn, _NT, preferred_element_type=F32)
    ct, st = cost_ref[...], sint_ref[...]
    scale = ATTN_HEAD_DIM ** -0.5 * LOG2E
    for h in range(ATTN_HEADS):
        blk = qt[h * ATTN_HEAD_DIM:(h + 1) * ATTN_HEAD_DIM]
        x1, x2 = blk[0:ROT_HALF], blk[ROT_HALF:ROT_DIM]
        r = jnp.concatenate([x1 * ct - x2 * st, x2 * ct + x1 * st, blk[ROT_DIM:]], axis=0) * scale
        qt_ref[0, h * ATTN_HEAD_DIM:(h + 1) * ATTN_HEAD_DIM, :] = r.astype(BF16)

    vt = lax.dot_general(wvt_ref[...], xn, _NT, preferred_element_type=F32)
    for t in range(rows // MOBA_BLOCK):
        vt_ref[0, t] = vt[:, t * MOBA_BLOCK:(t + 1) * MOBA_BLOCK].astype(BF16)


def _attn_proj(x, norm_w, wqt, wk, wvt, cn, sa, sb, cost, sint):
    B, S, D = x.shape
    rows = PROJ_ROWS
    nb_step = rows // MOBA_BLOCK
    const = lambda b, i: (0, 0)
    return pl.pallas_call(
        _attn_proj_kernel,
        grid=(B, S // rows),
        in_specs=[
            pl.BlockSpec((1, rows, D), lambda b, i: (b, i, 0)),
            pl.BlockSpec((1, D), const),
            pl.BlockSpec((D, D), const),
            pl.BlockSpec((D, D), const),
            pl.BlockSpec((D, D), const),
            pl.BlockSpec((rows, LANES), lambda b, i: (i, 0)),
            pl.BlockSpec((rows, LANES), lambda b, i: (i, 0)),
            pl.BlockSpec((rows, LANES), lambda b, i: (i, 0)),
            pl.BlockSpec((ROT_HALF, rows), lambda b, i: (0, i)),
            pl.BlockSpec((ROT_HALF, rows), lambda b, i: (0, i)),
        ],
        out_specs=[
            pl.BlockSpec((1, D, rows), lambda b, i: (b, 0, i)),
            pl.BlockSpec((1, rows, D), lambda b, i: (b, i, 0)),
            pl.BlockSpec((1, nb_step, D, MOBA_BLOCK), lambda b, i: (b, i, 0, 0)),
            pl.BlockSpec((1, 1, nb_step, D), lambda b, i: (b, i, 0, 0)),
        ],
        out_shape=[
            jax.ShapeDtypeStruct((B, D, S), BF16),
            jax.ShapeDtypeStruct((B, S, D), BF16),
            jax.ShapeDtypeStruct((B, S // MOBA_BLOCK, D, MOBA_BLOCK), BF16),
            jax.ShapeDtypeStruct((B, S // rows, nb_step, D), F32),
        ],
        compiler_params=_params(2),
        name="attn_proj",
    )(x, norm_w, wqt, wk, wvt, cn, sa, sb, cost, sint)


def _moba_kernel(qt_ref, k_ref, vt_ref, km_ref, o_ref, sel_ref, s_sc):
    cur = pl.program_id(2)
    nblk = km_ref.shape[1]
    blk_q = qt_ref.shape[2]
    hd = ATTN_HEAD_DIM
    heads = range(MOBA_HEADS)
    row = lax.broadcasted_iota(jnp.int32, (LANES, blk_q), 0)
    blk = lax.broadcasted_iota(jnp.int32, (nblk, blk_q), 0)
    kpos = lax.broadcasted_iota(jnp.int32, (MOBA_BLOCK, blk_q), 0)
    qpos = lax.broadcasted_iota(jnp.int32, (MOBA_BLOCK, blk_q), 1)
    ones_rows = jnp.ones((16, MOBA_BLOCK), BF16)

    qpads = []
    for a in heads:
        g, r = a // 2, a % 2
        qt2 = qt_ref[0, g * LANES:(g + 1) * LANES, :]
        qta = jnp.where((row >= r * hd) & (row < (r + 1) * hd), qt2, jnp.zeros_like(qt2))
        qpads.append(qta)
        km = km_ref[0, :, g * LANES:(g + 1) * LANES].astype(BF16)
        gate = jnp.dot(km, qta, preferred_element_type=F32)
        gate = jnp.where(blk < cur, gate, NEG_INF)
        sel = jnp.zeros((nblk, blk_q), F32)
        for _ in range(MOBA_TOPK):
            mx = jnp.max(gate, axis=0, keepdims=True)
            idx = jnp.min(jnp.where(gate == mx, blk, nblk), axis=0, keepdims=True)
            hit = blk == idx
            sel = jnp.where(hit & (blk < cur), 1.0, sel)
            gate = jnp.where(hit, NEG_INF, gate)
        sel_ref[a] = sel

    def scores(j):
        jj = jnp.minimum(j, nblk - 1)
        rows = pl.ds(pl.multiple_of(jj * MOBA_BLOCK, MOBA_BLOCK), MOBA_BLOCK)
        return [jnp.dot(k_ref[0, rows, (a // 2) * LANES:(a // 2 + 1) * LANES], qpads[a],
                        preferred_element_type=F32) for a in heads]

    def v_ext(j, a):
        return jnp.concatenate([vt_ref[0, j, a * hd:(a + 1) * hd, :], ones_rows], axis=0)

    m_acc = []
    for a, s in enumerate(scores(cur)):
        s = jnp.where(kpos <= qpos, s, NEG_INF)
        m = jnp.max(s, axis=0, keepdims=True)
        p = jnp.exp2(s - m).astype(BF16)
        m_acc += [m, jnp.dot(v_ext(cur, a), p, preferred_element_type=F32)]

    def park(s, slot):
        for a in heads:
            s_sc[slot, a] = s[a]
        return [jnp.max(s[a], axis=0, keepdims=True) for a in heads]

    def consume(j, slot, smax, m_acc):
        jj = jnp.minimum(j, nblk - 1)
        out = []
        for a in heads:
            m, acc = m_acc[2 * a], m_acc[2 * a + 1]
            chosen = sel_ref[a, pl.ds(jj, 1), :] > 0.5
            m_new = jnp.where(chosen, jnp.maximum(m, smax[a]), m)
            p = jnp.exp2(s_sc[slot, a] - jnp.where(chosen, m_new, jnp.inf)).astype(BF16)
            acc = jnp.exp2(m - m_new) * acc + jnp.dot(v_ext(jj, a), p, preferred_element_type=F32)
            out += [m_new, acc]
        return out

    nh = MOBA_HEADS

    def body(i, carry):
        m_acc, smax0, smax1 = list(carry[:2 * nh]), list(carry[2 * nh:3 * nh]), list(carry[3 * nh:])
        s_next = scores(2 * i + 2)
        m_acc = consume(2 * i, 0, smax0, m_acc)
        smax0 = park(s_next, 0)
        s_next = scores(2 * i + 3)
        m_acc = consume(2 * i + 1, 1, smax1, m_acc)
        smax1 = park(s_next, 1)
        return tuple(m_acc + smax0 + smax1)

    init = tuple(m_acc + park(scores(0), 0) + park(scores(1), 1))
    carry = lax.fori_loop(0, (cur + 1) // 2, body, init)
    outs = [carry[2 * a + 1][:hd] / carry[2 * a + 1][hd:hd + 1] for a in heads]
    o_ref[0] = jnp.concatenate(outs, axis=0).T.astype(BF16)


def _moba(qt, k, vt, kmean):
    B, D, S = qt.shape
    nblk = S // MOBA_BLOCK
    width = MOBA_HEADS * ATTN_HEAD_DIM
    return pl.pallas_call(
        _moba_kernel,
        grid=(B, D // width, nblk),
        in_specs=[
            pl.BlockSpec((1, width, MOBA_BLOCK), lambda b, p, i: (b, p, i)),
            pl.BlockSpec((1, S, width), lambda b, p, i: (b, 0, p)),
            pl.BlockSpec((1, nblk, width, MOBA_BLOCK), lambda b, p, i: (b, 0, p, 0)),
            pl.BlockSpec((1, nblk, width), lambda b, p, i: (b, 0, p)),
        ],
        out_specs=pl.BlockSpec((1, MOBA_BLOCK, width), lambda b, p, i: (b, i, p)),
        out_shape=jax.ShapeDtypeStruct((B, S, D), BF16),
        scratch_shapes=[pltpu.VMEM((MOBA_HEADS, nblk, MOBA_BLOCK), F32),
                        pltpu.VMEM((2, MOBA_HEADS, MOBA_BLOCK, MOBA_BLOCK), F32)],
        compiler_params=_params(3),
        name="moba",
    )(qt, k, vt, kmean)


def _proj_ffn_kernel(a_ref, wp_ref, res_ref, nw_ref, wg_ref, wu_ref, wd_ref, fw_ref, o_ref,
                     h_sc, xn_sc, *, final_norm):
    f = pl.program_id(1)

    @pl.when(f == 0)
    def _():
        h = jnp.dot(a_ref[...], wp_ref[...], preferred_element_type=F32) + res_ref[...]
        h_sc[...] = h
        xn_sc[...] = _rms(h, nw_ref[...]).astype(BF16)

    xn = xn_sc[...]
    g = jnp.dot(xn, wg_ref[...], preferred_element_type=F32)
    u = jnp.dot(xn, wu_ref[...], preferred_element_type=F32)
    act = (g * jax.nn.sigmoid(g) * u).astype(BF16)
    h_sc[...] += jnp.dot(act, wd_ref[...], preferred_element_type=F32)

    @pl.when(f == pl.num_programs(1) - 1)
    def _():
        h = h_sc[...]
        o_ref[...] = _rms(h, fw_ref[...]) if final_norm else h


def _proj_ffn(a, wp, res, norm_w, wgu, wd, final_w, *, final_norm):
    T, D = res.shape
    rows, cols = FFN_ROWS, FFN_COLS
    nf = D_FF // cols
    const = lambda i, f: (0, 0)
    return pl.pallas_call(
        functools.partial(_proj_ffn_kernel, final_norm=final_norm),
        grid=(T // rows, nf),
        in_specs=[
            pl.BlockSpec((rows, D), lambda i, f: (i, 0)),
            pl.BlockSpec((D, D), const),
            pl.BlockSpec((rows, D), lambda i, f: (i, 0)),
            pl.BlockSpec((1, D), const),
            pl.BlockSpec((D, cols), lambda i, f: (0, f)),
            pl.BlockSpec((D, cols), lambda i, f: (0, nf + f)),
            pl.BlockSpec((cols, D), lambda i, f: (f, 0)),
            pl.BlockSpec((1, D), const),
        ],
        out_specs=pl.BlockSpec((rows, D), lambda i, f: (i, 0)),
        out_shape=jax.ShapeDtypeStruct((T, D), F32),
        scratch_shapes=[pltpu.VMEM((rows, D), F32), pltpu.VMEM((rows, D), BF16)],
        compiler_params=_params(2),
        name="proj_ffn_final" if final_norm else "proj_ffn",
    )(a, wp, res, norm_w, wgu, wgu, wd, final_w)


def _mlstm_proj_kernel(x_ref, nw_ref, wq_ref, wk_ref, wkt_ref, wv_ref, wo_ref, wgt_ref, bg_ref,
                       q_ref, k_ref, kt_ref, v_ref, og_ref, gt_ref):
    xn = _rms(x_ref[0], nw_ref[...]).astype(BF16)
    scale = MLSTM_QK_DIM ** -0.5
    q_ref[0] = (jnp.dot(xn, wq_ref[...], preferred_element_type=F32) * scale).astype(BF16)
    k_ref[0] = jnp.dot(xn, wk_ref[...], preferred_element_type=F32).astype(BF16)
    kt_ref[0] = lax.dot_general(wkt_ref[...], xn, _NT, preferred_element_type=F32).astype(BF16)
    v_ref[0] = jnp.dot(xn, wv_ref[...], preferred_element_type=F32).astype(BF16)
    og_ref[0] = jax.nn.sigmoid(jnp.dot(xn, wo_ref[...], preferred_element_type=F32)).astype(BF16)
    gates = lax.dot_general(wgt_ref[...], xn, _NT, preferred_element_type=F32) + bg_ref[...]
    gt_ref[0] = GATE_SOFTCAP * jnp.tanh(gates / GATE_SOFTCAP)


def _mlstm_proj(x, norm_w, wq, wk, wkt, wv, wo, wgt, bg):
    B, S, D = x.shape
    rows = PROJ_ROWS
    H, dk = MLSTM_HEADS, MLSTM_QK_DIM
    const = lambda b, i: (0, 0)
    tok = lambda w: pl.BlockSpec((1, rows, w), lambda b, i: (b, i, 0))
    return pl.pallas_call(
        _mlstm_proj_kernel,
        grid=(B, S // rows),
        in_specs=[
            tok(D),
            pl.BlockSpec((1, D), const),
            pl.BlockSpec((D, H * dk), const),
            pl.BlockSpec((D, H * dk), const),
            pl.BlockSpec((H * dk, D), const),
            pl.BlockSpec((D, D), const),
            pl.BlockSpec((D, D), const),
            pl.BlockSpec((2 * H, D), const),
            pl.BlockSpec((2 * H, 1), const),
        ],
        out_specs=[
            tok(H * dk), tok(H * dk),
            pl.BlockSpec((1, H * dk, rows), lambda b, i: (b, 0, i)),
            tok(D), tok(D),
            pl.BlockSpec((1, 2 * H, rows), lambda b, i: (b, 0, i)),
        ],
        out_shape=[
            jax.ShapeDtypeStruct((B, S, H * dk), BF16),
            jax.ShapeDtypeStruct((B, S, H * dk), BF16),
            jax.ShapeDtypeStruct((B, H * dk, S), BF16),
            jax.ShapeDtypeStruct((B, S, D), BF16),
            jax.ShapeDtypeStruct((B, S, D), BF16),
            jax.ShapeDtypeStruct((B, 2 * H, S), F32),
        ],
        compiler_params=_params(2),
        name="mlstm_proj",
    )(x, norm_w, wq, wk, wkt, wv, wo, wgt, bg)


def _mlstm_core_kernel(q_ref, k_ref, kt_ref, v_ref, og_ref, gt_ref, hw_ref, y_ref, c_sc, m_sc):
    L = MLSTM_CHUNK
    H, dk, dv = MLSTM_HEADS, MLSTM_QK_DIM, MLSTM_V_DIM

    @pl.when(pl.program_id(1) == 0)
    def _():
        c_sc[...] = jnp.zeros_like(c_sc)
        m_sc[...] = jnp.zeros_like(m_sc)

    lane8 = lax.broadcasted_iota(jnp.int32, (H, L), 1)
    l_idx = lax.broadcasted_iota(jnp.int32, (L, L), 0)
    s_idx = lax.broadcasted_iota(jnp.int32, (L, L), 1)
    causal = s_idx <= l_idx
    lane = lax.broadcasted_iota(jnp.int32, (L, LANES), 1)
    one_col = jnp.where(lane == 0, 1.0, 0.0).astype(BF16)
    hw = hw_ref[...]

    for c in range(q_ref.shape[1] // L):
        rows = slice(c * L, (c + 1) * L)
        gates = gt_ref[0, :, rows]
        ipre, fpre = gates[0:H], gates[H:2 * H]
        logf = jnp.minimum(fpre, 0.0) - jnp.log1p(jnp.exp(-jnp.abs(fpre)))
        b = logf
        sh = 1
        while sh < L:
            b = b + jnp.where(lane8 >= sh, pltpu.roll(b, sh, 1), 0.0)
            sh *= 2
        u = ipre - b

        for hp in range(H // 2):
            q2 = q_ref[0, rows, hp * LANES:(hp + 1) * LANES]
            k2 = k_ref[0, rows, hp * LANES:(hp + 1) * LANES]
            for a in range(2):
                h = 2 * hp + a
                qa = jnp.where((lane >= a * dk) & (lane < (a + 1) * dk), q2, jnp.zeros_like(q2))
                s = lax.dot_general(qa, k2, _NT, preferred_element_type=F32)
                uh, lfh = u[h:h + 1], logf[h:h + 1]
                m_prev = m_sc[h:h + 1, 0:1]
                um = jnp.where(causal, uh, NEG_INF)
                big_m = jnp.maximum(jnp.max(um, axis=1, keepdims=True), m_prev)
                e = jnp.exp(um - big_m)
                b_col = jnp.sum(jnp.where(causal, lfh, 0.0), axis=1, keepdims=True)
                inter = jnp.exp(m_prev - big_m)
                floor = jnp.exp(-(b_col + big_m))
                v_ext = jnp.concatenate([v_ref[0, rows, h * dv:(h + 1) * dv], one_col], axis=1)
                c_pair = c_sc[hp]
                r = (inter * jnp.dot(qa, c_pair.astype(BF16), preferred_element_type=F32)
                     + jnp.dot((s * e).astype(BF16), v_ext, preferred_element_type=F32))
                hh = r[:, :dv] / jnp.maximum(jnp.abs(r[:, dv:dv + 1]), floor)
                hn = _rms(hh, hw[:, h * dv:(h + 1) * dv])
                y_ref[0, rows, h * dv:(h + 1) * dv] = (
                    og_ref[0, rows, h * dv:(h + 1) * dv].astype(F32) * hn).astype(BF16)

                m_last = jnp.maximum(jnp.max(uh, axis=1, keepdims=True), m_prev)
                w = jnp.exp(uh - m_last)
                kw = (kt_ref[0, h * dk:(h + 1) * dk, rows].astype(F32) * w).astype(BF16)
                c_sc[hp, a * dk:(a + 1) * dk, :] = (
                    jnp.exp(m_prev - m_last) * c_pair[a * dk:(a + 1) * dk]
                    + jnp.dot(kw, v_ext, preferred_element_type=F32))
                m_sc[h:h + 1, :] = jnp.broadcast_to(b[h:h + 1, L - 1:L] + m_last, (1, LANES))


def _mlstm_core(q, k, kt, v, og, gt, head_w):
    B, S, D = v.shape
    rows = MLSTM_ROWS
    H, dk, dv = MLSTM_HEADS, MLSTM_QK_DIM, MLSTM_V_DIM
    tok = lambda w: pl.BlockSpec((1, rows, w), lambda b, i: (b, i, 0))
    return pl.pallas_call(
        _mlstm_core_kernel,
        grid=(B, S // rows),
        in_specs=[
            tok(H * dk), tok(H * dk),
            pl.BlockSpec((1, H * dk, rows), lambda b, i: (b, 0, i)),
            tok(D), tok(D),
            pl.BlockSpec((1, 2 * H, rows), lambda b, i: (b, 0, i)),
            pl.BlockSpec((1, D), lambda b, i: (0, 0)),
        ],
        out_specs=tok(D),
        out_shape=jax.ShapeDtypeStruct((B, S, D), BF16),
        scratch_shapes=[pltpu.VMEM((H // 2, 2 * dk, 2 * dv), F32), pltpu.VMEM((H, LANES), F32)],
        compiler_params=_params(2),
        name="mlstm_core",
    )(q, k, kt, v, og, gt, head_w)


def _rotary_tables(seq):
    pos = jnp.arange(seq, dtype=F32)
    inv_freq = ROPE_THETA ** (-jnp.arange(0, ROT_DIM, 2, dtype=F32) / ROT_DIM)
    ang = pos[:, None] * inv_freq[None, :]
    cos, sin = jnp.cos(ang), jnp.sin(ang)
    d = jnp.arange(LANES) % ATTN_HEAD_DIM
    cos_l, sin_l = cos[:, d % ROT_HALF], sin[:, d % ROT_HALF]
    cn = jnp.where(d < ROT_DIM, cos_l, 1.0)
    sa = jnp.where(d < ROT_HALF, -sin_l, 0.0)
    sb = jnp.where((d >= ROT_HALF) & (d < ROT_DIM), sin_l, 0.0)
    return cn, sa, sb, cos.T, sin.T


def kernel(x, attn_norm, attn_w_qkv, attn_w_o, mlstm_norm, mlstm_w_in, mlstm_b_gates, mlstm_head_norm,
           mlstm_w_out, ffn_norm, ffn_w_gate_up, ffn_w_down, final_norm):
    B, S, D = x.shape
    T = B * S
    H, dk = MLSTM_HEADS, MLSTM_QK_DIM
    row = lambda w: w.reshape(1, -1).astype(F32)

    cn, sa, sb, cost, sint = _rotary_tables(S)
    wqkv = attn_w_qkv[0].astype(BF16)
    qt, k, vt, kmean = _attn_proj(x, row(attn_norm[0]), wqkv[:, :D].T, wqkv[:, D:2 * D], wqkv[:, 2 * D:].T,
                                  cn, sa, sb, cost, sint)
    o = _moba(qt, k, vt, kmean.reshape(B, S // MOBA_BLOCK, D))
    h = _proj_ffn(o.reshape(T, D), attn_w_o[0].astype(BF16), x.reshape(T, D), row(ffn_norm[0]),
                  ffn_w_gate_up[0].astype(BF16), ffn_w_down[0].astype(BF16), row(final_norm),
                  final_norm=False)

    w_in = mlstm_w_in[0].astype(BF16)
    q_end, k_end, v_end, o_end = H * dk, 2 * H * dk, 2 * H * dk + D, 2 * H * dk + 2 * D
    q, k, kt, v, og, gt = _mlstm_proj(
        h.reshape(B, S, D), row(mlstm_norm[0]), w_in[:, :q_end], w_in[:, q_end:k_end],
        w_in[:, q_end:k_end].T, w_in[:, k_end:v_end], w_in[:, v_end:o_end], w_in[:, o_end:].T,
        mlstm_b_gates[0].reshape(2 * H, 1).astype(F32))
    y = _mlstm_core(q, k, kt, v, og, gt, row(mlstm_head_norm[0]))
    out = _proj_ffn(y.reshape(T, D), mlstm_w_out[0].astype(BF16), h, row(ffn_norm[1]),
                    ffn_w_gate_up[1].astype(BF16), ffn_w_down[1].astype(BF16), row(final_norm),
                    final_norm=True)
    return out.reshape(B, S, D)
```

```python
import functools

import jax
import jax.numpy as jnp
from jax import lax
from jax.experimental import pallas as pl
from jax.experimental.pallas import tpu as pltpu

F32 = jnp.float32
BF16 = jnp.bfloat16

D_MODEL = 1024
ATTN_HEADS = 16
ATTN_HEAD_DIM = 64
ROT_DIM = 16
ROT_HALF = ROT_DIM // 2
ROPE_THETA = 500000.0
MOBA_BLOCK = 256
MOBA_TOPK = 3
MLSTM_HEADS = 8
MLSTM_QK_DIM = 64
MLSTM_V_DIM = 128
GATE_SOFTCAP = 15.0
D_FF = 2816
RMS_EPS = 1e-6

LANES = 128
VMEM_LIMIT_BYTES = 56 * 1024 * 1024
PROJ_ROWS = 512
FFN_ROWS = 512
FFN_COLS = D_FF // 2
MLSTM_CHUNK = 256
MLSTM_ROWS = 512
MOBA_HEADS = 4
LOG2E = 1.4426950408889634
NEG_INF = float("-inf")

_NT = (((1,), (1,)), ((), ()))


def _params(n_grid):
    return pltpu.CompilerParams(dimension_semantics=("arbitrary",) * n_grid,
                                vmem_limit_bytes=VMEM_LIMIT_BYTES)


def _rms(x, w):
    return x * lax.rsqrt(jnp.mean(x * x, axis=-1, keepdims=True) + RMS_EPS) * w


def _attn_proj_kernel(x_ref, nw_ref, wqt_ref, wk_ref, wvt_ref, cn_ref, sa_ref, sb_ref, cost_ref,
                      sint_ref, qt_ref, k_ref, vt_ref, kmean_ref):
    rows = x_ref.shape[1]
    xn = _rms(x_ref[0], nw_ref[...]).astype(BF16)

    k = jnp.dot(xn, wk_ref[...], preferred_element_type=F32)
    cn, sa, sb = cn_ref[...], sa_ref[...], sb_ref[...]
    for g in range(D_MODEL // LANES):
        kg = k[:, g * LANES:(g + 1) * LANES]
        kr = (kg * cn + pltpu.roll(kg, LANES - ROT_HALF, 1) * sa + pltpu.roll(kg, ROT_HALF, 1) * sb)
        k_ref[0, :, g * LANES:(g + 1) * LANES] = kr.astype(BF16)
        for t in range(rows // MOBA_BLOCK):
            kmean_ref[0, 0, t:t + 1, g * LANES:(g + 1) * LANES] = jnp.mean(
                kr[t * MOBA_BLOCK:(t + 1) * MOBA_BLOCK], axis=0, keepdims=True)

    qt = lax.dot_general(wqt_ref[...], xn, _NT, preferred_element_type=F32)
    ct, st = cost_ref[...], sint_ref[...]
    scale = ATTN_HEAD_DIM ** -0.5 * LOG2E
    for h in range(ATTN_HEADS):
        blk = qt[h * ATTN_HEAD_DIM:(h + 1) * ATTN_HEAD_DIM]
        x1, x2 = blk[0:ROT_HALF], blk[ROT_HALF:ROT_DIM]
        r = jnp.concatenate([x1 * ct - x2 * st, x2 * ct + x1 * st, blk[ROT_DIM:]], axis=0) * scale
        qt_ref[0, h * ATTN_HEAD_DIM:(h + 1) * ATTN_HEAD_DIM, :] = r.astype(BF16)

    vt = lax.dot_general(wvt_ref[...], xn, _NT, preferred_element_type=F32)
    for t in range(rows // MOBA_BLOCK):
        vt_ref[0, t] = vt[:, t * MOBA_BLOCK:(t + 1) * MOBA_BLOCK].astype(BF16)


def _attn_proj(x, norm_w, wqt, wk, wvt, cn, sa, sb, cost, sint):
    B, S, D = x.shape
    rows = PROJ_ROWS
    nb_step = rows // MOBA_BLOCK
    const = lambda b, i: (0, 0)
    return pl.pallas_call(
        _attn_proj_kernel,
        grid=(B, S // rows),
        in_specs=[
            pl.BlockSpec((1, rows, D), lambda b, i: (b, i, 0)),
            pl.BlockSpec((1, D), const),
            pl.BlockSpec((D, D), const),
            pl.BlockSpec((D, D), const),
            pl.BlockSpec((D, D), const),
            pl.BlockSpec((rows, LANES), lambda b, i: (i, 0)),
            pl.BlockSpec((rows, LANES), lambda b, i: (i, 0)),
            pl.BlockSpec((rows, LANES), lambda b, i: (i, 0)),
            pl.BlockSpec((ROT_HALF, rows), lambda b, i: (0, i)),
            pl.BlockSpec((ROT_HALF, rows), lambda b, i: (0, i)),
        ],
        out_specs=[
            pl.BlockSpec((1, D, rows), lambda b, i: (b, 0, i)),
            pl.BlockSpec((1, rows, D), lambda b, i: (b, i, 0)),
            pl.BlockSpec((1, nb_step, D, MOBA_BLOCK), lambda b, i: (b, i, 0, 0)),
            pl.BlockSpec((1, 1, nb_step, D), lambda b, i: (b, i, 0, 0)),
        ],
        out_shape=[
            jax.ShapeDtypeStruct((B, D, S), BF16),
            jax.ShapeDtypeStruct((B, S, D), BF16),
            jax.ShapeDtypeStruct((B, S // MOBA_BLOCK, D, MOBA_BLOCK), BF16),
            jax.ShapeDtypeStruct((B, S // rows, nb_step, D), F32),
        ],
        compiler_params=_params(2),
        name="attn_proj",
    )(x, norm_w, wqt, wk, wvt, cn, sa, sb, cost, sint)


def _moba_kernel(qt_ref, k_ref, vt_ref, km_ref, o_ref, sel_ref, s_sc):
    cur = pl.program_id(2)
    nblk = km_ref.shape[1]
    blk_q = qt_ref.shape[2]
    hd = ATTN_HEAD_DIM
    heads = range(MOBA_HEADS)
    row = lax.broadcasted_iota(jnp.int32, (LANES, blk_q), 0)
    blk = lax.broadcasted_iota(jnp.int32, (nblk, blk_q), 0)
    kpos = lax.broadcasted_iota(jnp.int32, (MOBA_BLOCK, blk_q), 0)
    qpos = lax.broadcasted_iota(jnp.int32, (MOBA_BLOCK, blk_q), 1)
    ones_rows = jnp.ones((16, MOBA_BLOCK), BF16)

    qpads = []
    for a in heads:
        g, r = a // 2, a % 2
        qt2 = qt_ref[0, g * LANES:(g + 1) * LANES, :]
        qta = jnp.where((row >= r * hd) & (row < (r + 1) * hd), qt2, jnp.zeros_like(qt2))
        qpads.append(qta)
        km = km_ref[0, :, g * LANES:(g + 1) * LANES].astype(BF16)
        gate = jnp.dot(km, qta, preferred_element_type=F32)
        gate = jnp.where(blk < cur, gate, NEG_INF)
        sel = jnp.zeros((nblk, blk_q), F32)
        for _ in range(MOBA_TOPK):
            mx = jnp.max(gate, axis=0, keepdims=True)
            idx = jnp.min(jnp.where(gate == mx, blk, nblk), axis=0, keepdims=True)
            hit = blk == idx
            sel = jnp.where(hit & (blk < cur), 1.0, sel)
            gate = jnp.where(hit, NEG_INF, gate)
        sel_ref[a] = sel

    def scores(j):
        jj = jnp.minimum(j, nblk - 1)
        rows = pl.ds(pl.multiple_of(jj * MOBA_BLOCK, MOBA_BLOCK), MOBA_BLOCK)
        return [jnp.dot(k_ref[0, rows, (a // 2) * LANES:(a // 2 + 1) * LANES], qpads[a],
                        preferred_element_type=F32) for a in heads]

    def v_ext(j, a):
        return jnp.concatenate([vt_ref[0, j, a * hd:(a + 1) * hd, :], ones_rows], axis=0)

    m_acc = []
    for a, s in enumerate(scores(cur)):
        s = jnp.where(kpos <= qpos, s, NEG_INF)
        m = jnp.max(s, axis=0, keepdims=True)
        p = jnp.exp2(s - m).astype(BF16)
        m_acc += [m, jnp.dot(v_ext(cur, a), p, preferred_element_type=F32)]

    def park(s, slot):
        for a in heads:
            s_sc[slot, a] = s[a]
        return [jnp.max(s[a], axis=0, keepdims=True) for a in heads]

    def consume(j, slot, smax, m_acc):
        jj = jnp.minimum(j, nblk - 1)
        out = []
        for a in heads:
            m, acc = m_acc[2 * a], m_acc[2 * a + 1]
            chosen = sel_ref[a, pl.ds(jj, 1), :] > 0.5
            m_new = jnp.where(chosen, jnp.maximum(m, smax[a]), m)
            p = jnp.exp2(s_sc[slot, a] - jnp.where(chosen, m_new, jnp.inf)).astype(BF16)
            acc = jnp.exp2(m - m_new) * acc + jnp.dot(v_ext(jj, a), p, preferred_element_type=F32)
            out += [m_new, acc]
        return out

    nh = MOBA_HEADS

    def body(i, carry):
        m_acc, smax0, smax1 = list(carry[:2 * nh]), list(carry[2 * nh:3 * nh]), list(carry[3 * nh:])
        s_next = scores(2 * i + 2)
        m_acc = consume(2 * i, 0, smax0, m_acc)
        smax0 = park(s_next, 0)
        s_next = scores(2 * i + 3)
        m_acc = consume(2 * i + 1, 1, smax1, m_acc)
        smax1 = park(s_next, 1)
        return tuple(m_acc + smax0 + smax1)

    init = tuple(m_acc + park(scores(0), 0) + park(scores(1), 1))
    carry = lax.fori_loop(0, (cur + 1) // 2, body, init)
    outs = [carry[2 * a + 1][:hd] / carry[2 * a + 1][hd:hd + 1] for a in heads]
    o_ref[0] = jnp.concatenate(outs, axis=0).T.astype(BF16)


def _moba(qt, k, vt, kmean):
    B, D, S = qt.shape
    nblk = S // MOBA_BLOCK
    width = MOBA_HEADS * ATTN_HEAD_DIM
    return pl.pallas_call(
        _moba_kernel,
        grid=(B, D // width, nblk),
        in_specs=[
            pl.BlockSpec((1, width, MOBA_BLOCK), lambda b, p, i: (b, p, i)),
            pl.BlockSpec((1, S, width), lambda b, p, i: (b, 0, p)),
            pl.BlockSpec((1, nblk, width, MOBA_BLOCK), lambda b, p, i: (b, 0, p, 0)),
            pl.BlockSpec((1, nblk, width), lambda b, p, i: (b, 0, p)),
        ],
        out_specs=pl.BlockSpec((1, MOBA_BLOCK, width), lambda b, p, i: (b, i, p)),
        out_shape=jax.ShapeDtypeStruct((B, S, D), BF16),
        scratch_shapes=[pltpu.VMEM((MOBA_HEADS, nblk, MOBA_BLOCK), F32),
                        pltpu.VMEM((2, MOBA_HEADS, MOBA_BLOCK, MOBA_BLOCK), F32)],
        compiler_params=_params(3),
        name="moba",
    )(qt, k, vt, kmean)


def _proj_ffn_kernel(a_ref, wp_ref, res_ref, nw_ref, wg_ref, wu_ref, wd_ref, fw_ref, o_ref,
                     h_sc, xn_sc, *, final_norm):
    f = pl.program_id(1)

    @pl.when(f == 0)
    def _():
        h = jnp.dot(a_ref[...], wp_ref[...], preferred_element_type=F32) + res_ref[...]
        h_sc[...] = h
        xn_sc[...] = _rms(h, nw_ref[...]).astype(BF16)

    xn = xn_sc[...]
    g = jnp.dot(xn, wg_ref[...], preferred_element_type=F32)
    u = jnp.dot(xn, wu_ref[...], preferred_element_type=F32)
    act = (g * jax.nn.sigmoid(g) * u).astype(BF16)
    h_sc[...] += jnp.dot(act, wd_ref[...], preferred_element_type=F32)

    @pl.when(f == pl.num_programs(1) - 1)
    def _():
        h = h_sc[...]
        o_ref[...] = _rms(h, fw_ref[...]) if final_norm else h


def _proj_ffn(a, wp, res, norm_w, wgu, wd, final_w, *, final_norm):
    T, D = res.shape
    rows, cols = FFN_ROWS, FFN_COLS
    nf = D_FF // cols
    const = lambda i, f: (0, 0)
    return pl.pallas_call(
        functools.partial(_proj_ffn_kernel, final_norm=final_norm),
        grid=(T // rows, nf),
        in_specs=[
            pl.BlockSpec((rows, D), lambda i, f: (i, 0)),
            pl.BlockSpec((D, D), const),
            pl.BlockSpec((rows, D), lambda i, f: (i, 0)),
            pl.BlockSpec((1, D), const),
            pl.BlockSpec((D, cols), lambda i, f: (0, f)),
            pl.BlockSpec((D, cols), lambda i, f: (0, nf + f)),
            pl.BlockSpec((cols, D), lambda i, f: (f, 0)),
            pl.BlockSpec((1, D), const),
        ],
        out_specs=pl.BlockSpec((rows, D), lambda i, f: (i, 0)),
        out_shape=jax.ShapeDtypeStruct((T, D), F32),
        scratch_shapes=[pltpu.VMEM((rows, D), F32), pltpu.VMEM((rows, D), BF16)],
        compiler_params=_params(2),
        name="proj_ffn_final" if final_norm else "proj_ffn",
    )(a, wp, res, norm_w, wgu, wgu, wd, final_w)


def _mlstm_proj_kernel(x_ref, nw_ref, wqt_ref, wk_ref, wvt_ref, wot_ref, wgt_ref, bg_ref,
                       qt_ref, k_ref, vt_ref, ogt_ref, gt_ref):
    xn = _rms(x_ref[0], nw_ref[...]).astype(BF16)
    scale = MLSTM_QK_DIM ** -0.5
    nt = lambda w_ref: lax.dot_general(w_ref[...], xn, _NT, preferred_element_type=F32)
    qt_ref[0] = (nt(wqt_ref) * scale).astype(BF16)
    k_ref[0] = jnp.dot(xn, wk_ref[...], preferred_element_type=F32).astype(BF16)
    vt_ref[0] = nt(wvt_ref).astype(BF16)
    ogt_ref[0] = jax.nn.sigmoid(nt(wot_ref)).astype(BF16)
    gates = nt(wgt_ref) + bg_ref[...]
    gt_ref[0] = GATE_SOFTCAP * jnp.tanh(gates / GATE_SOFTCAP)


def _mlstm_proj(x, norm_w, wqt, wk, wvt, wot, wgt, bg):
    B, S, D = x.shape
    rows = PROJ_ROWS
    H, dk = MLSTM_HEADS, MLSTM_QK_DIM
    const = lambda b, i: (0, 0)
    tok = lambda w: pl.BlockSpec((1, rows, w), lambda b, i: (b, i, 0))
    tok_t = lambda w: pl.BlockSpec((1, w, rows), lambda b, i: (b, 0, i))
    return pl.pallas_call(
        _mlstm_proj_kernel,
        grid=(B, S // rows),
        in_specs=[
            tok(D),
            pl.BlockSpec((1, D), const),
            pl.BlockSpec((H * dk, D), const),
            pl.BlockSpec((D, H * dk), const),
            pl.BlockSpec((D, D), const),
            pl.BlockSpec((D, D), const),
            pl.BlockSpec((2 * H, D), const),
            pl.BlockSpec((2 * H, 1), const),
        ],
        out_specs=[tok_t(H * dk), tok(H * dk), tok_t(D), tok_t(D), tok_t(2 * H)],
        out_shape=[
            jax.ShapeDtypeStruct((B, H * dk, S), BF16),
            jax.ShapeDtypeStruct((B, S, H * dk), BF16),
            jax.ShapeDtypeStruct((B, D, S), BF16),
            jax.ShapeDtypeStruct((B, D, S), BF16),
            jax.ShapeDtypeStruct((B, 2 * H, S), F32),
        ],
        compiler_params=_params(2),
        name="mlstm_proj",
    )(x, norm_w, wqt, wk, wvt, wot, wgt, bg)


def _lane_scan(x, op, fill):
    lane = lax.broadcasted_iota(jnp.int32, (x.shape[0], LANES), 1)
    groups, carry = [], None
    for t in range(x.shape[1] // LANES):
        y = x[:, t * LANES:(t + 1) * LANES]
        sh = 1
        while sh < LANES:
            y = op(y, jnp.where(lane >= sh, pltpu.roll(y, sh, 1), fill))
            sh *= 2
        if carry is not None:
            y = op(y, carry)
        carry = y[:, LANES - 1:LANES]
        groups.append(y)
    return jnp.concatenate(groups, axis=1)


def _mlstm_core_kernel(qt_ref, k_ref, vt_ref, ogt_ref, gt_ref, hw_ref, y_ref, c_sc, m_sc):
    L = MLSTM_CHUNK
    H, dk, dv = MLSTM_HEADS, MLSTM_QK_DIM, MLSTM_V_DIM

    @pl.when(pl.program_id(1) == 0)
    def _():
        c_sc[...] = jnp.zeros_like(c_sc)
        m_sc[...] = jnp.zeros_like(m_sc)

    s_idx = lax.broadcasted_iota(jnp.int32, (L, L), 0)
    l_idx = lax.broadcasted_iota(jnp.int32, (L, L), 1)
    causal = s_idx <= l_idx
    feat = lax.broadcasted_iota(jnp.int32, (LANES, L), 0)
    one_rows = jnp.where(lax.broadcasted_iota(jnp.int32, (16, L), 0) == 0, 1.0, 0.0).astype(BF16)

    for c in range(qt_ref.shape[2] // L):
        cols = slice(c * L, (c + 1) * L)
        gates = gt_ref[0, :, cols]
        ipre, fpre = gates[0:H], gates[H:2 * H]
        logf = jnp.minimum(fpre, 0.0) - jnp.log1p(jnp.exp(-jnp.abs(fpre)))
        b = _lane_scan(logf, jnp.add, 0.0)
        u = ipre - b
        m_prev = m_sc[:, 0:1]
        big_m = jnp.maximum(_lane_scan(u, jnp.maximum, NEG_INF), m_prev)
        inter = jnp.exp(m_prev - big_m)
        floor = jnp.exp(-(b + big_m))
        m_last = big_m[:, L - 1:L]
        w = jnp.exp(u - m_last)
        decay = jnp.exp(m_prev - m_last)
        m_sc[...] = jnp.broadcast_to(b[:, L - 1:L] + m_last, (H, LANES))
        u_cols = jnp.concatenate([u * LOG2E, jnp.zeros((LANES - H, L), F32)], axis=0).T
        big_m2 = big_m * LOG2E

        for hp in range(H // 2):
            k2 = k_ref[0, cols, hp * LANES:(hp + 1) * LANES]
            qt2 = qt_ref[0, hp * LANES:(hp + 1) * LANES, cols]
            for a in range(2):
                h = 2 * hp + a
                qta = jnp.where((feat >= a * dk) & (feat < (a + 1) * dk), qt2, jnp.zeros_like(qt2))
                s_t = jnp.dot(k2, qta, preferred_element_type=F32)
                e_t = jnp.exp2(jnp.where(causal, u_cols[:, h:h + 1] - big_m2[h:h + 1], NEG_INF))
                v_ext = jnp.concatenate([vt_ref[0, h * dv:(h + 1) * dv, cols], one_rows], axis=0)
                state = c_sc[h]
                r = (inter[h:h + 1] * jnp.dot(state.astype(BF16), qta, preferred_element_type=F32)
                     + jnp.dot(v_ext, (s_t * e_t).astype(BF16), preferred_element_type=F32))
                hh = r[:dv] / jnp.maximum(jnp.abs(r[dv:dv + 1]), floor[h:h + 1])
                hn = hh * lax.rsqrt(jnp.mean(hh * hh, axis=0, keepdims=True) + RMS_EPS)
                hw = hw_ref[h * dv:(h + 1) * dv, :]
                hn = hn * jnp.concatenate([hw] * (L // LANES), axis=1)
                y_t = ogt_ref[0, h * dv:(h + 1) * dv, cols].astype(F32) * hn
                y_ref[0, cols, h * dv:(h + 1) * dv] = y_t.T.astype(BF16)

                wv = (v_ext.astype(F32) * w[h:h + 1]).astype(BF16)
                c_sc[h] = decay[h:h + 1] * state + jnp.dot(wv, k2, preferred_element_type=F32)


def _mlstm_core(qt, k, vt, ogt, gt, head_w):
    B, D, S = vt.shape
    rows = MLSTM_ROWS
    H, dk, dv = MLSTM_HEADS, MLSTM_QK_DIM, MLSTM_V_DIM
    tok_t = lambda w: pl.BlockSpec((1, w, rows), lambda b, i: (b, 0, i))
    return pl.pallas_call(
        _mlstm_core_kernel,
        grid=(B, S // rows),
        in_specs=[
            tok_t(H * dk),
            pl.BlockSpec((1, rows, H * dk), lambda b, i: (b, i, 0)),
            tok_t(D), tok_t(D), tok_t(2 * H),
            pl.BlockSpec((D, LANES), lambda b, i: (0, 0)),
        ],
        out_specs=pl.BlockSpec((1, rows, D), lambda b, i: (b, i, 0)),
        out_shape=jax.ShapeDtypeStruct((B, S, D), BF16),
        scratch_shapes=[pltpu.VMEM((H, dv + 16, LANES), F32), pltpu.VMEM((H, LANES), F32)],
        compiler_params=_params(2),
        name="mlstm_core",
    )(qt, k, vt, ogt, gt, head_w)


def _rotary_tables(seq):
    pos = jnp.arange(seq, dtype=F32)
    inv_freq = ROPE_THETA ** (-jnp.arange(0, ROT_DIM, 2, dtype=F32) / ROT_DIM)
    ang = pos[:, None] * inv_freq[None, :]
    cos, sin = jnp.cos(ang), jnp.sin(ang)
    d = jnp.arange(LANES) % ATTN_HEAD_DIM
    cos_l, sin_l = cos[:, d % ROT_HALF], sin[:, d % ROT_HALF]
    cn = jnp.where(d < ROT_DIM, cos_l, 1.0)
    sa = jnp.where(d < ROT_HALF, -sin_l, 0.0)
    sb = jnp.where((d >= ROT_HALF) & (d < ROT_DIM), sin_l, 0.0)
    return cn, sa, sb, cos.T, sin.T


def kernel(x, attn_norm, attn_w_qkv, attn_w_o, mlstm_norm, mlstm_w_in, mlstm_b_gates, mlstm_head_norm,
           mlstm_w_out, ffn_norm, ffn_w_gate_up, ffn_w_down, final_norm):
    B, S, D = x.shape
    T = B * S
    H, dk = MLSTM_HEADS, MLSTM_QK_DIM
    row = lambda w: w.reshape(1, -1).astype(F32)

    cn, sa, sb, cost, sint = _rotary_tables(S)
    wqkv = attn_w_qkv[0].astype(BF16)
    qt, k, vt, kmean = _attn_proj(x, row(attn_norm[0]), wqkv[:, :D].T, wqkv[:, D:2 * D], wqkv[:, 2 * D:].T,
                                  cn, sa, sb, cost, sint)
    o = _moba(qt, k, vt, kmean.reshape(B, S // MOBA_BLOCK, D))
    h = _proj_ffn(o.reshape(T, D), attn_w_o[0].astype(BF16), x.reshape(T, D), row(ffn_norm[0]),
                  ffn_w_gate_up[0].astype(BF16), ffn_w_down[0].astype(BF16), row(final_norm),
                  final_norm=False)

    w_in = mlstm_w_in[0].astype(BF16)
    q_end, k_end, v_end, o_end = H * dk, 2 * H * dk, 2 * H * dk + D, 2 * H * dk + 2 * D
    qt, k, vt, ogt, gt = _mlstm_proj(
        h.reshape(B, S, D), row(mlstm_norm[0]), w_in[:, :q_end].T, w_in[:, q_end:k_end],
        w_in[:, k_end:v_end].T, w_in[:, v_end:o_end].T, w_in[:, o_end:].T,
        mlstm_b_gates[0].reshape(2 * H, 1).astype(F32))
    head_w = jnp.broadcast_to(mlstm_head_norm[0].astype(F32).reshape(D, 1), (D, LANES))
    y = _mlstm_core(qt, k, vt, ogt, gt, head_w)
    out = _proj_ffn(y.reshape(T, D), mlstm_w_out[0].astype(BF16), h, row(ffn_norm[1]),
                    ffn_w_gate_up[1].astype(BF16), ffn_w_down[1].astype(BF16), row(final_norm),
                    final_norm=True)
    return out.reshape(B, S, D)
```

```python
import functools

import jax
import jax.numpy as jnp
from jax import lax
from jax.experimental import pallas as pl
from jax.experimental.pallas import tpu as pltpu

F32 = jnp.float32
BF16 = jnp.bfloat16

D_MODEL = 1024
ATTN_HEADS = 16
ATTN_HEAD_DIM = 64
ROT_DIM = 16
ROT_HALF = ROT_DIM // 2
ROPE_THETA = 500000.0
MOBA_BLOCK = 256
MOBA_TOPK = 3
MLSTM_HEADS = 8
MLSTM_QK_DIM = 64
MLSTM_V_DIM = 128
GATE_SOFTCAP = 15.0
D_FF = 2816
RMS_EPS = 1e-6

LANES = 128
VMEM_LIMIT_BYTES = 56 * 1024 * 1024
PROJ_ROWS = 512
FFN_ROWS = 512
MLSTM_CHUNK = 256
MLSTM_ROWS = 512
MOBA_HEADS = 4
LOG2E = 1.4426950408889634
NEG_INF = float("-inf")

_NT = (((1,), (1,)), ((), ()))


def _params(n_grid):
    return pltpu.CompilerParams(dimension_semantics=("arbitrary",) * n_grid,
                                vmem_limit_bytes=VMEM_LIMIT_BYTES)


def _rms(x, w):
    return x * lax.rsqrt(jnp.mean(x * x, axis=-1, keepdims=True) + RMS_EPS) * w


def _attn_proj_kernel(x_ref, nw_ref, wqt_ref, wk_ref, wvt_ref, cn_ref, sa_ref, sb_ref, cost_ref,
                      sint_ref, qt_ref, k_ref, vt_ref, kmean_ref):
    rows = x_ref.shape[1]
    xn = _rms(x_ref[0], nw_ref[...]).astype(BF16)

    k = jnp.dot(xn, wk_ref[...], preferred_element_type=F32)
    cn, sa, sb = cn_ref[...], sa_ref[...], sb_ref[...]
    for g in range(D_MODEL // LANES):
        kg = k[:, g * LANES:(g + 1) * LANES]
        kr = (kg * cn + pltpu.roll(kg, LANES - ROT_HALF, 1) * sa + pltpu.roll(kg, ROT_HALF, 1) * sb)
        k_ref[0, :, g * LANES:(g + 1) * LANES] = kr.astype(BF16)
        for t in range(rows // MOBA_BLOCK):
            kmean_ref[0, 0, t:t + 1, g * LANES:(g + 1) * LANES] = jnp.mean(
                kr[t * MOBA_BLOCK:(t + 1) * MOBA_BLOCK], axis=0, keepdims=True)

    qt = lax.dot_general(wqt_ref[...], xn, _NT, preferred_element_type=F32)
    ct, st = cost_ref[...], sint_ref[...]
    scale = ATTN_HEAD_DIM ** -0.5 * LOG2E
    for h in range(ATTN_HEADS):
        blk = qt[h * ATTN_HEAD_DIM:(h + 1) * ATTN_HEAD_DIM]
        x1, x2 = blk[0:ROT_HALF], blk[ROT_HALF:ROT_DIM]
        r = jnp.concatenate([x1 * ct - x2 * st, x2 * ct + x1 * st, blk[ROT_DIM:]], axis=0) * scale
        qt_ref[0, h * ATTN_HEAD_DIM:(h + 1) * ATTN_HEAD_DIM, :] = r.astype(BF16)

    vt = lax.dot_general(wvt_ref[...], xn, _NT, preferred_element_type=F32)
    for t in range(rows // MOBA_BLOCK):
        vt_ref[0, t] = vt[:, t * MOBA_BLOCK:(t + 1) * MOBA_BLOCK].astype(BF16)


def _attn_proj(x, norm_w, wqt, wk, wvt, cn, sa, sb, cost, sint):
    B, S, D = x.shape
    rows = PROJ_ROWS
    nb_step = rows // MOBA_BLOCK
    const = lambda b, i: (0, 0)
    return pl.pallas_call(
        _attn_proj_kernel,
        grid=(B, S // rows),
        in_specs=[
            pl.BlockSpec((1, rows, D), lambda b, i: (b, i, 0)),
            pl.BlockSpec((1, D), const),
            pl.BlockSpec((D, D), const),
            pl.BlockSpec((D, D), const),
            pl.BlockSpec((D, D), const),
            pl.BlockSpec((rows, LANES), lambda b, i: (i, 0)),
            pl.BlockSpec((rows, LANES), lambda b, i: (i, 0)),
            pl.BlockSpec((rows, LANES), lambda b, i: (i, 0)),
            pl.BlockSpec((ROT_HALF, rows), lambda b, i: (0, i)),
            pl.BlockSpec((ROT_HALF, rows), lambda b, i: (0, i)),
        ],
        out_specs=[
            pl.BlockSpec((1, D, rows), lambda b, i: (b, 0, i)),
            pl.BlockSpec((1, rows, D), lambda b, i: (b, i, 0)),
            pl.BlockSpec((1, nb_step, D, MOBA_BLOCK), lambda b, i: (b, i, 0, 0)),
            pl.BlockSpec((1, 1, nb_step, D), lambda b, i: (b, i, 0, 0)),
        ],
        out_shape=[
            jax.ShapeDtypeStruct((B, D, S), BF16),
            jax.ShapeDtypeStruct((B, S, D), BF16),
            jax.ShapeDtypeStruct((B, S // MOBA_BLOCK, D, MOBA_BLOCK), BF16),
            jax.ShapeDtypeStruct((B, S // rows, nb_step, D), F32),
        ],
        compiler_params=_params(2),
        name="attn_proj",
    )(x, norm_w, wqt, wk, wvt, cn, sa, sb, cost, sint)


def _moba_kernel(qt_ref, k_ref, vt_ref, km_ref, o_ref, sel_ref, s_sc):
    cur = pl.program_id(2)
    nblk = km_ref.shape[1]
    blk_q = qt_ref.shape[2]
    hd = ATTN_HEAD_DIM
    heads = range(MOBA_HEADS)
    row = lax.broadcasted_iota(jnp.int32, (LANES, blk_q), 0)
    blk = lax.broadcasted_iota(jnp.int32, (nblk, blk_q), 0)
    kpos = lax.broadcasted_iota(jnp.int32, (MOBA_BLOCK, blk_q), 0)
    qpos = lax.broadcasted_iota(jnp.int32, (MOBA_BLOCK, blk_q), 1)
    ones_rows = jnp.ones((16, MOBA_BLOCK), BF16)

    qpads = []
    for a in heads:
        g, r = a // 2, a % 2
        qt2 = qt_ref[0, g * LANES:(g + 1) * LANES, :]
        qta = jnp.where((row >= r * hd) & (row < (r + 1) * hd), qt2, jnp.zeros_like(qt2))
        qpads.append(qta)
        km = km_ref[0, :, g * LANES:(g + 1) * LANES].astype(BF16)
        gate = jnp.dot(km, qta, preferred_element_type=F32)
        gate = jnp.where(blk < cur, gate, NEG_INF)
        sel = jnp.zeros((nblk, blk_q), F32)
        for _ in range(MOBA_TOPK):
            mx = jnp.max(gate, axis=0, keepdims=True)
            idx = jnp.min(jnp.where(gate == mx, blk, nblk), axis=0, keepdims=True)
            hit = blk == idx
            sel = jnp.where(hit & (blk < cur), 1.0, sel)
            gate = jnp.where(hit, NEG_INF, gate)
        sel_ref[a] = sel

    def scores(j):
        jj = jnp.minimum(j, nblk - 1)
        rows = pl.ds(pl.multiple_of(jj * MOBA_BLOCK, MOBA_BLOCK), MOBA_BLOCK)
        return [jnp.dot(k_ref[0, rows, (a // 2) * LANES:(a // 2 + 1) * LANES], qpads[a],
                        preferred_element_type=F32) for a in heads]

    def v_ext(j, a):
        return jnp.concatenate([vt_ref[0, j, a * hd:(a + 1) * hd, :], ones_rows], axis=0)

    m_acc = []
    for a, s in enumerate(scores(cur)):
        s = jnp.where(kpos <= qpos, s, NEG_INF)
        m = jnp.max(s, axis=0, keepdims=True)
        p = jnp.exp2(s - m).astype(BF16)
        m_acc += [m, jnp.dot(v_ext(cur, a), p, preferred_element_type=F32)]

    def park(s, slot):
        for a in heads:
            s_sc[slot, a] = s[a]
        return [jnp.max(s[a], axis=0, keepdims=True) for a in heads]

    def consume(j, slot, smax, m_acc):
        jj = jnp.minimum(j, nblk - 1)
        out = []
        for a in heads:
            m, acc = m_acc[2 * a], m_acc[2 * a + 1]
            chosen = sel_ref[a, pl.ds(jj, 1), :] > 0.5
            m_new = jnp.where(chosen, jnp.maximum(m, smax[a]), m)
            p = jnp.exp2(s_sc[slot, a] - jnp.where(chosen, m_new, jnp.inf)).astype(BF16)
            acc = jnp.exp2(m - m_new) * acc + jnp.dot(v_ext(jj, a), p, preferred_element_type=F32)
            out += [m_new, acc]
        return out

    nh = MOBA_HEADS

    def body(i, carry):
        m_acc, smax0, smax1 = list(carry[:2 * nh]), list(carry[2 * nh:3 * nh]), list(carry[3 * nh:])
        s_next = scores(2 * i + 2)
        m_acc = consume(2 * i, 0, smax0, m_acc)
        smax0 = park(s_next, 0)
        s_next = scores(2 * i + 3)
        m_acc = consume(2 * i + 1, 1, smax1, m_acc)
        smax1 = park(s_next, 1)
        return tuple(m_acc + smax0 + smax1)

    init = tuple(m_acc + park(scores(0), 0) + park(scores(1), 1))
    carry = lax.fori_loop(0, (cur + 1) // 2, body, init)
    outs = [carry[2 * a + 1][:hd] / carry[2 * a + 1][hd:hd + 1] for a in heads]
    o_ref[0] = jnp.concatenate(outs, axis=0).T.astype(BF16)


def _moba(qt, k, vt, kmean):
    B, D, S = qt.shape
    nblk = S // MOBA_BLOCK
    width = MOBA_HEADS * ATTN_HEAD_DIM
    return pl.pallas_call(
        _moba_kernel,
        grid=(B, D // width, nblk),
        in_specs=[
            pl.BlockSpec((1, width, MOBA_BLOCK), lambda b, p, i: (b, p, i)),
            pl.BlockSpec((1, S, width), lambda b, p, i: (b, 0, p)),
            pl.BlockSpec((1, nblk, width, MOBA_BLOCK), lambda b, p, i: (b, 0, p, 0)),
            pl.BlockSpec((1, nblk, width), lambda b, p, i: (b, 0, p)),
        ],
        out_specs=pl.BlockSpec((1, MOBA_BLOCK, width), lambda b, p, i: (b, i, p)),
        out_shape=jax.ShapeDtypeStruct((B, S, D), BF16),
        scratch_shapes=[pltpu.VMEM((MOBA_HEADS, nblk, MOBA_BLOCK), F32),
                        pltpu.VMEM((2, MOBA_HEADS, MOBA_BLOCK, MOBA_BLOCK), F32)],
        compiler_params=_params(3),
        name="moba",
    )(qt, k, vt, kmean)


def _proj_ffn_kernel(a_ref, wp_ref, res_ref, nw_ref, wg_ref, wu_ref, wd_ref, fw_ref, o_ref, *, final_norm):
    h = jnp.dot(a_ref[...], wp_ref[0], preferred_element_type=F32) + res_ref[...]
    xn = _rms(h, nw_ref[...]).astype(BF16)
    g = jnp.dot(xn, wg_ref[0], preferred_element_type=F32)
    u = jnp.dot(xn, wu_ref[0], preferred_element_type=F32)
    act = (g * jax.nn.sigmoid(g) * u).astype(BF16)
    h = h + jnp.dot(act, wd_ref[0], preferred_element_type=F32)
    o_ref[...] = _rms(h, fw_ref[...]) if final_norm else h


def _proj_ffn(a, wp, res, norm_w, wgu, wd, final_w, *, layer, final_norm):
    T, D = res.shape
    rows = FFN_ROWS
    once = pl.Buffered(1)
    return pl.pallas_call(
        functools.partial(_proj_ffn_kernel, final_norm=final_norm),
        grid=(T // rows,),
        in_specs=[
            pl.BlockSpec((rows, D), lambda i: (i, 0)),
            pl.BlockSpec((1, D, D), lambda i: (0, 0, 0), pipeline_mode=once),
            pl.BlockSpec((rows, D), lambda i: (i, 0)),
            pl.BlockSpec((1, D), lambda i: (0, 0)),
            pl.BlockSpec((1, D, D_FF), lambda i: (layer, 0, 0), pipeline_mode=once),
            pl.BlockSpec((1, D, D_FF), lambda i: (layer, 0, 1), pipeline_mode=once),
            pl.BlockSpec((1, D_FF, D), lambda i: (layer, 0, 0), pipeline_mode=once),
            pl.BlockSpec((1, D), lambda i: (0, 0)),
        ],
        out_specs=pl.BlockSpec((rows, D), lambda i: (i, 0)),
        out_shape=jax.ShapeDtypeStruct((T, D), F32),
        compiler_params=_params(1),
        name="proj_ffn_final" if final_norm else "proj_ffn",
    )(a, wp, res, norm_w, wgu, wgu, wd, final_w)


def _mlstm_proj_kernel(x_ref, nw_ref, wqt_ref, wk_ref, wvt_ref, wot_ref, wgt_ref, bg_ref,
                       qt_ref, k_ref, vt_ref, ogt_ref, gt_ref):
    xn = _rms(x_ref[0], nw_ref[...]).astype(BF16)
    scale = MLSTM_QK_DIM ** -0.5
    nt = lambda w_ref: lax.dot_general(w_ref[...], xn, _NT, preferred_element_type=F32)
    qt_ref[0] = (nt(wqt_ref) * scale).astype(BF16)
    k_ref[0] = jnp.dot(xn, wk_ref[...], preferred_element_type=F32).astype(BF16)
    vt_ref[0] = nt(wvt_ref).astype(BF16)
    ogt_ref[0] = jax.nn.sigmoid(nt(wot_ref)).astype(BF16)
    gates = nt(wgt_ref) + bg_ref[...]
    gt_ref[0] = GATE_SOFTCAP * jnp.tanh(gates / GATE_SOFTCAP)


def _mlstm_proj(x, norm_w, wqt, wk, wvt, wot, wgt, bg):
    B, S, D = x.shape
    rows = PROJ_ROWS
    H, dk = MLSTM_HEADS, MLSTM_QK_DIM
    const = lambda b, i: (0, 0)
    tok = lambda w: pl.BlockSpec((1, rows, w), lambda b, i: (b, i, 0))
    tok_t = lambda w: pl.BlockSpec((1, w, rows), lambda b, i: (b, 0, i))
    return pl.pallas_call(
        _mlstm_proj_kernel,
        grid=(B, S // rows),
        in_specs=[
            tok(D),
            pl.BlockSpec((1, D), const),
            pl.BlockSpec((H * dk, D), const),
            pl.BlockSpec((D, H * dk), const),
            pl.BlockSpec((D, D), const),
            pl.BlockSpec((D, D), const),
            pl.BlockSpec((2 * H, D), const),
            pl.BlockSpec((2 * H, 1), const),
        ],
        out_specs=[tok_t(H * dk), tok(H * dk), tok_t(D), tok_t(D), tok_t(2 * H)],
        out_shape=[
            jax.ShapeDtypeStruct((B, H * dk, S), BF16),
            jax.ShapeDtypeStruct((B, S, H * dk), BF16),
            jax.ShapeDtypeStruct((B, D, S), BF16),
            jax.ShapeDtypeStruct((B, D, S), BF16),
            jax.ShapeDtypeStruct((B, 2 * H, S), F32),
        ],
        compiler_params=_params(2),
        name="mlstm_proj",
    )(x, norm_w, wqt, wk, wvt, wot, wgt, bg)


def _lane_scan(x, op, fill):
    lane = lax.broadcasted_iota(jnp.int32, (x.shape[0], LANES), 1)
    groups, carry = [], None
    for t in range(x.shape[1] // LANES):
        y = x[:, t * LANES:(t + 1) * LANES]
        sh = 1
        while sh < LANES:
            y = op(y, jnp.where(lane >= sh, pltpu.roll(y, sh, 1), fill))
            sh *= 2
        if carry is not None:
            y = op(y, carry)
        carry = y[:, LANES - 1:LANES]
        groups.append(y)
    return jnp.concatenate(groups, axis=1)


def _mlstm_core_kernel(qt_ref, k_ref, vt_ref, ogt_ref, gt_ref, hw_ref, y_ref, c_sc, m_sc):
    L = MLSTM_CHUNK
    H, dk, dv = MLSTM_HEADS, MLSTM_QK_DIM, MLSTM_V_DIM

    @pl.when(pl.program_id(1) == 0)
    def _():
        c_sc[...] = jnp.zeros_like(c_sc)
        m_sc[...] = jnp.zeros_like(m_sc)

    s_idx = lax.broadcasted_iota(jnp.int32, (L, L), 0)
    l_idx = lax.broadcasted_iota(jnp.int32, (L, L), 1)
    causal = s_idx <= l_idx
    feat = lax.broadcasted_iota(jnp.int32, (LANES, L), 0)
    one_rows = jnp.where(lax.broadcasted_iota(jnp.int32, (16, L), 0) == 0, 1.0, 0.0).astype(BF16)

    for c in range(qt_ref.shape[2] // L):
        cols = slice(c * L, (c + 1) * L)
        gates = gt_ref[0, :, cols]
        ipre, fpre = gates[0:H], gates[H:2 * H]
        logf = jnp.minimum(fpre, 0.0) - jnp.log1p(jnp.exp(-jnp.abs(fpre)))
        b = _lane_scan(logf, jnp.add, 0.0)
        u = ipre - b
        m_prev = m_sc[:, 0:1]
        big_m = jnp.maximum(_lane_scan(u, jnp.maximum, NEG_INF), m_prev)
        inter = jnp.exp(m_prev - big_m)
        floor = jnp.exp(-(b + big_m))
        m_last = big_m[:, L - 1:L]
        w = jnp.exp(u - m_last)
        decay = jnp.exp(m_prev - m_last)
        m_sc[...] = jnp.broadcast_to(b[:, L - 1:L] + m_last, (H, LANES))
        u_cols = jnp.concatenate([u * LOG2E, jnp.zeros((LANES - H, L), F32)], axis=0).T
        big_m2 = big_m * LOG2E

        for hp in range(H // 2):
            k2 = k_ref[0, cols, hp * LANES:(hp + 1) * LANES]
            qt2 = qt_ref[0, hp * LANES:(hp + 1) * LANES, cols]
            for a in range(2):
                h = 2 * hp + a
                qta = jnp.where((feat >= a * dk) & (feat < (a + 1) * dk), qt2, jnp.zeros_like(qt2))
                s_t = jnp.dot(k2, qta, preferred_element_type=F32)
                e_t = jnp.exp2(jnp.where(causal, u_cols[:, h:h + 1] - big_m2[h:h + 1], NEG_INF))
                v_ext = jnp.concatenate([vt_ref[0, h * dv:(h + 1) * dv, cols], one_rows], axis=0)
                state = c_sc[h]
                r = (inter[h:h + 1] * jnp.dot(state.astype(BF16), qta, preferred_element_type=F32)
                     + jnp.dot(v_ext, (s_t * e_t).astype(BF16), preferred_element_type=F32))
                hh = r[:dv] / jnp.maximum(jnp.abs(r[dv:dv + 1]), floor[h:h + 1])
                hn = hh * lax.rsqrt(jnp.mean(hh * hh, axis=0, keepdims=True) + RMS_EPS)
                hw = hw_ref[h * dv:(h + 1) * dv, :]
                hn = hn * jnp.concatenate([hw] * (L // LANES), axis=1)
                y_t = ogt_ref[0, h * dv:(h + 1) * dv, cols].astype(F32) * hn
                y_ref[0, cols, h * dv:(h + 1) * dv] = y_t.T.astype(BF16)

                wv = (v_ext.astype(F32) * w[h:h + 1]).astype(BF16)
                c_sc[h] = decay[h:h + 1] * state + jnp.dot(wv, k2, preferred_element_type=F32)


def _mlstm_core(qt, k, vt, ogt, gt, head_w):
    B, D, S = vt.shape
    rows = MLSTM_ROWS
    H, dk, dv = MLSTM_HEADS, MLSTM_QK_DIM, MLSTM_V_DIM
    tok_t = lambda w: pl.BlockSpec((1, w, rows), lambda b, i: (b, 0, i))
    return pl.pallas_call(
        _mlstm_core_kernel,
        grid=(B, S // rows),
        in_specs=[
            tok_t(H * dk),
            pl.BlockSpec((1, rows, H * dk), lambda b, i: (b, i, 0)),
            tok_t(D), tok_t(D), tok_t(2 * H),
            pl.BlockSpec((D, LANES), lambda b, i: (0, 0)),
        ],
        out_specs=pl.BlockSpec((1, rows, D), lambda b, i: (b, i, 0)),
        out_shape=jax.ShapeDtypeStruct((B, S, D), BF16),
        scratch_shapes=[pltpu.VMEM((H, dv + 16, LANES), F32), pltpu.VMEM((H, LANES), F32)],
        compiler_params=_params(2),
        name="mlstm_core",
    )(qt, k, vt, ogt, gt, head_w)


def _rotary_tables(seq):
    pos = jnp.arange(seq, dtype=F32)
    inv_freq = ROPE_THETA ** (-jnp.arange(0, ROT_DIM, 2, dtype=F32) / ROT_DIM)
    ang = pos[:, None] * inv_freq[None, :]
    cos, sin = jnp.cos(ang), jnp.sin(ang)
    d = jnp.arange(LANES) % ATTN_HEAD_DIM
    cos_l, sin_l = cos[:, d % ROT_HALF], sin[:, d % ROT_HALF]
    cn = jnp.where(d < ROT_DIM, cos_l, 1.0)
    sa = jnp.where(d < ROT_HALF, -sin_l, 0.0)
    sb = jnp.where((d >= ROT_HALF) & (d < ROT_DIM), sin_l, 0.0)
    return cn, sa, sb, cos.T, sin.T


def kernel(x, attn_norm, attn_w_qkv, attn_w_o, mlstm_norm, mlstm_w_in, mlstm_b_gates, mlstm_head_norm,
           mlstm_w_out, ffn_norm, ffn_w_gate_up, ffn_w_down, final_norm):
    B, S, D = x.shape
    T = B * S
    H, dk = MLSTM_HEADS, MLSTM_QK_DIM
    row = lambda w: w.reshape(1, -1).astype(F32)

    cn, sa, sb, cost, sint = _rotary_tables(S)
    wqkv = attn_w_qkv[0].astype(BF16)
    qt, k, vt, kmean = _attn_proj(x, row(attn_norm[0]), wqkv[:, :D].T, wqkv[:, D:2 * D], wqkv[:, 2 * D:].T,
                                  cn, sa, sb, cost, sint)
    o = _moba(qt, k, vt, kmean.reshape(B, S // MOBA_BLOCK, D))
    wgu, wd = ffn_w_gate_up.astype(BF16), ffn_w_down.astype(BF16)
    h = _proj_ffn(o.reshape(T, D), attn_w_o.astype(BF16), x.reshape(T, D), row(ffn_norm[0]), wgu, wd,
                  row(final_norm), layer=0, final_norm=False)

    w_in = mlstm_w_in[0].astype(BF16)
    q_end, k_end, v_end, o_end = H * dk, 2 * H * dk, 2 * H * dk + D, 2 * H * dk + 2 * D
    qt, k, vt, ogt, gt = _mlstm_proj(
        h.reshape(B, S, D), row(mlstm_norm[0]), w_in[:, :q_end].T, w_in[:, q_end:k_end],
        w_in[:, k_end:v_end].T, w_in[:, v_end:o_end].T, w_in[:, o_end:].T,
        mlstm_b_gates[0].reshape(2 * H, 1).astype(F32))
    head_w = jnp.broadcast_to(mlstm_head_norm[0].astype(F32).reshape(D, 1), (D, LANES))
    y = _mlstm_core(qt, k, vt, ogt, gt, head_w)
    out = _proj_ffn(y.reshape(T, D), mlstm_w_out.astype(BF16), h, row(ffn_norm[1]), wgu, wd,
                    row(final_norm), layer=1, final_norm=True)
    return out.reshape(B, S, D)
```

```python
import functools

import jax
import jax.numpy as jnp
from jax import lax
from jax.experimental import pallas as pl
from jax.experimental.pallas import tpu as pltpu

F32 = jnp.float32
BF16 = jnp.bfloat16

D_MODEL = 1024
ATTN_HEADS = 16
ATTN_HEAD_DIM = 64
ROT_DIM = 16
ROT_HALF = ROT_DIM // 2
ROPE_THETA = 500000.0
MOBA_BLOCK = 256
MOBA_TOPK = 3
MLSTM_HEADS = 8
MLSTM_QK_DIM = 64
MLSTM_V_DIM = 128
GATE_SOFTCAP = 15.0
D_FF = 2816
RMS_EPS = 1e-6

LANES = 128
VMEM_LIMIT_BYTES = 56 * 1024 * 1024
PROJ_ROWS = 512
FFN_ROWS = 512
MLSTM_CHUNK = 256
MLSTM_ROWS = 512
MOBA_HEADS = 4
MOBA_STEP = 4
LOG2E = 1.4426950408889634
NEG_INF = float("-inf")

_NT = (((1,), (1,)), ((), ()))


def _params(n_grid):
    return pltpu.CompilerParams(dimension_semantics=("arbitrary",) * n_grid,
                                vmem_limit_bytes=VMEM_LIMIT_BYTES)


def _rms(x, w):
    return x * lax.rsqrt(jnp.mean(x * x, axis=-1, keepdims=True) + RMS_EPS) * w


def _attn_proj_kernel(x_ref, nw_ref, wqt_ref, wk_ref, wvt_ref, cn_ref, sa_ref, sb_ref, cost_ref,
                      sint_ref, qt_ref, k_ref, vt_ref, kmean_ref):
    rows = x_ref.shape[1]
    xn = _rms(x_ref[0], nw_ref[...]).astype(BF16)

    k = jnp.dot(xn, wk_ref[...], preferred_element_type=F32)
    cn, sa, sb = cn_ref[...], sa_ref[...], sb_ref[...]
    for g in range(D_MODEL // LANES):
        kg = k[:, g * LANES:(g + 1) * LANES]
        kr = (kg * cn + pltpu.roll(kg, LANES - ROT_HALF, 1) * sa + pltpu.roll(kg, ROT_HALF, 1) * sb)
        k_ref[0, :, g * LANES:(g + 1) * LANES] = kr.astype(BF16)
        for t in range(rows // MOBA_BLOCK):
            kmean_ref[0, 0, t:t + 1, g * LANES:(g + 1) * LANES] = jnp.mean(
                kr[t * MOBA_BLOCK:(t + 1) * MOBA_BLOCK], axis=0, keepdims=True)

    qt = lax.dot_general(wqt_ref[...], xn, _NT, preferred_element_type=F32)
    ct, st = cost_ref[...], sint_ref[...]
    scale = ATTN_HEAD_DIM ** -0.5 * LOG2E
    for h in range(ATTN_HEADS):
        blk = qt[h * ATTN_HEAD_DIM:(h + 1) * ATTN_HEAD_DIM]
        x1, x2 = blk[0:ROT_HALF], blk[ROT_HALF:ROT_DIM]
        r = jnp.concatenate([x1 * ct - x2 * st, x2 * ct + x1 * st, blk[ROT_DIM:]], axis=0) * scale
        qt_ref[0, h * ATTN_HEAD_DIM:(h + 1) * ATTN_HEAD_DIM, :] = r.astype(BF16)

    vt = lax.dot_general(wvt_ref[...], xn, _NT, preferred_element_type=F32)
    for t in range(rows // MOBA_BLOCK):
        vt_ref[0, t] = vt[:, t * MOBA_BLOCK:(t + 1) * MOBA_BLOCK].astype(BF16)


def _attn_proj(x, norm_w, wqt, wk, wvt, cn, sa, sb, cost, sint):
    B, S, D = x.shape
    rows = PROJ_ROWS
    nb_step = rows // MOBA_BLOCK
    const = lambda b, i: (0, 0)
    return pl.pallas_call(
        _attn_proj_kernel,
        grid=(B, S // rows),
        in_specs=[
            pl.BlockSpec((1, rows, D), lambda b, i: (b, i, 0)),
            pl.BlockSpec((1, D), const),
            pl.BlockSpec((D, D), const),
            pl.BlockSpec((D, D), const),
            pl.BlockSpec((D, D), const),
            pl.BlockSpec((rows, LANES), lambda b, i: (i, 0)),
            pl.BlockSpec((rows, LANES), lambda b, i: (i, 0)),
            pl.BlockSpec((rows, LANES), lambda b, i: (i, 0)),
            pl.BlockSpec((ROT_HALF, rows), lambda b, i: (0, i)),
            pl.BlockSpec((ROT_HALF, rows), lambda b, i: (0, i)),
        ],
        out_specs=[
            pl.BlockSpec((1, D, rows), lambda b, i: (b, 0, i)),
            pl.BlockSpec((1, rows, D), lambda b, i: (b, i, 0)),
            pl.BlockSpec((1, nb_step, D, MOBA_BLOCK), lambda b, i: (b, i, 0, 0)),
            pl.BlockSpec((1, 1, nb_step, D), lambda b, i: (b, i, 0, 0)),
        ],
        out_shape=[
            jax.ShapeDtypeStruct((B, D, S), BF16),
            jax.ShapeDtypeStruct((B, S, D), BF16),
            jax.ShapeDtypeStruct((B, S // MOBA_BLOCK, D, MOBA_BLOCK), BF16),
            jax.ShapeDtypeStruct((B, S // rows, nb_step, D), F32),
        ],
        compiler_params=_params(2),
        name="attn_proj",
    )(x, norm_w, wqt, wk, wvt, cn, sa, sb, cost, sint)


def _moba_kernel(qt_ref, k_ref, vt_ref, km_ref, o_ref, sel_ref, s_sc):
    cur = pl.program_id(2)
    nblk = km_ref.shape[1]
    blk_q = qt_ref.shape[2]
    hd = ATTN_HEAD_DIM
    heads = range(MOBA_HEADS)
    row = lax.broadcasted_iota(jnp.int32, (LANES, blk_q), 0)
    blk = lax.broadcasted_iota(jnp.int32, (nblk, blk_q), 0)
    kpos = lax.broadcasted_iota(jnp.int32, (MOBA_BLOCK, blk_q), 0)
    qpos = lax.broadcasted_iota(jnp.int32, (MOBA_BLOCK, blk_q), 1)
    ones_rows = jnp.ones((16, MOBA_BLOCK), BF16)

    qpads = []
    for a in heads:
        g, r = a // 2, a % 2
        qt2 = qt_ref[0, g * LANES:(g + 1) * LANES, :]
        qta = jnp.where((row >= r * hd) & (row < (r + 1) * hd), qt2, jnp.zeros_like(qt2))
        qpads.append(qta)
        km = km_ref[0, :, g * LANES:(g + 1) * LANES].astype(BF16)
        gate = jnp.dot(km, qta, preferred_element_type=F32)
        gate = jnp.where(blk < cur, gate, NEG_INF)
        sel = jnp.zeros((nblk, blk_q), F32)
        for _ in range(MOBA_TOPK):
            mx = jnp.max(gate, axis=0, keepdims=True)
            idx = jnp.min(jnp.where(gate == mx, blk, nblk), axis=0, keepdims=True)
            hit = blk == idx
            sel = jnp.where(hit & (blk < cur), 1.0, sel)
            gate = jnp.where(hit, NEG_INF, gate)
        sel_ref[a] = sel

    def scores(j):
        jj = jnp.minimum(j, nblk - 1)
        rows = pl.ds(pl.multiple_of(jj * MOBA_BLOCK, MOBA_BLOCK), MOBA_BLOCK)
        return [jnp.dot(k_ref[0, rows, (a // 2) * LANES:(a // 2 + 1) * LANES], qpads[a],
                        preferred_element_type=F32) for a in heads]

    def v_ext(j, a):
        return jnp.concatenate([vt_ref[0, j, a * hd:(a + 1) * hd, :], ones_rows], axis=0)

    m_acc = []
    for a, s in enumerate(scores(cur)):
        s = jnp.where(kpos <= qpos, s, NEG_INF)
        m = jnp.max(s, axis=0, keepdims=True)
        p = jnp.exp2(s - m).astype(BF16)
        m_acc += [m, jnp.dot(v_ext(cur, a), p, preferred_element_type=F32)]

    def park(s, slot):
        for a in heads:
            s_sc[slot, a] = s[a]
        return [jnp.max(s[a], axis=0, keepdims=True) for a in heads]

    def consume(j, slot, smax, m_acc):
        jj = jnp.minimum(j, nblk - 1)
        out = []
        for a in heads:
            m, acc = m_acc[2 * a], m_acc[2 * a + 1]
            chosen = sel_ref[a, pl.ds(jj, 1), :] > 0.5
            m_new = jnp.where(chosen, jnp.maximum(m, smax[a]), m)
            p = jnp.exp2(s_sc[slot, a] - jnp.where(chosen, m_new, jnp.inf)).astype(BF16)
            acc = jnp.exp2(m - m_new) * acc + jnp.dot(v_ext(jj, a), p, preferred_element_type=F32)
            out += [m_new, acc]
        return out

    nh = MOBA_HEADS

    def body(i, carry):
        m_acc, smax0, smax1 = list(carry[:2 * nh]), list(carry[2 * nh:3 * nh]), list(carry[3 * nh:])
        j = MOBA_STEP * i
        s_next = scores(j + 2)
        m_acc = consume(j, 0, smax0, m_acc)
        smax2 = park(s_next, 2)
        s_next = scores(j + 3)
        m_acc = consume(j + 1, 1, smax1, m_acc)
        smax3 = park(s_next, 3)
        s_next = scores(j + 4)
        m_acc = consume(j + 2, 2, smax2, m_acc)
        smax0 = park(s_next, 0)
        s_next = scores(j + 5)
        m_acc = consume(j + 3, 3, smax3, m_acc)
        smax1 = park(s_next, 1)
        return tuple(m_acc + smax0 + smax1)

    init = tuple(m_acc + park(scores(0), 0) + park(scores(1), 1))
    carry = lax.fori_loop(0, (cur + MOBA_STEP - 1) // MOBA_STEP, body, init)
    outs = [carry[2 * a + 1][:hd] / carry[2 * a + 1][hd:hd + 1] for a in heads]
    o_ref[0] = jnp.concatenate(outs, axis=0).T.astype(BF16)


def _moba(qt, k, vt, kmean):
    B, D, S = qt.shape
    nblk = S // MOBA_BLOCK
    width = MOBA_HEADS * ATTN_HEAD_DIM
    return pl.pallas_call(
        _moba_kernel,
        grid=(B, D // width, nblk),
        in_specs=[
            pl.BlockSpec((1, width, MOBA_BLOCK), lambda b, p, i: (b, p, i)),
            pl.BlockSpec((1, S, width), lambda b, p, i: (b, 0, p)),
            pl.BlockSpec((1, nblk, width, MOBA_BLOCK), lambda b, p, i: (b, 0, p, 0)),
            pl.BlockSpec((1, nblk, width), lambda b, p, i: (b, 0, p)),
        ],
        out_specs=pl.BlockSpec((1, MOBA_BLOCK, width), lambda b, p, i: (b, i, p)),
        out_shape=jax.ShapeDtypeStruct((B, S, D), BF16),
        scratch_shapes=[pltpu.VMEM((MOBA_HEADS, nblk, MOBA_BLOCK), F32),
                        pltpu.VMEM((MOBA_STEP, MOBA_HEADS, MOBA_BLOCK, MOBA_BLOCK), F32)],
        compiler_params=_params(3),
        name="moba",
    )(qt, k, vt, kmean)


def _proj_ffn_kernel(a_ref, wp_ref, res_ref, nw_ref, wg_ref, wu_ref, wd_ref, fw_ref, o_ref, *, final_norm):
    h = jnp.dot(a_ref[...], wp_ref[0], preferred_element_type=F32) + res_ref[...]
    xn = _rms(h, nw_ref[...]).astype(BF16)
    g = jnp.dot(xn, wg_ref[0], preferred_element_type=F32)
    u = jnp.dot(xn, wu_ref[0], preferred_element_type=F32)
    act = (g * jax.nn.sigmoid(g) * u).astype(BF16)
    h = h + jnp.dot(act, wd_ref[0], preferred_element_type=F32)
    o_ref[...] = _rms(h, fw_ref[...]) if final_norm else h


def _proj_ffn(a, wp, res, norm_w, wgu, wd, final_w, *, layer, final_norm):
    T, D = res.shape
    rows = FFN_ROWS
    once = pl.Buffered(1)
    return pl.pallas_call(
        functools.partial(_proj_ffn_kernel, final_norm=final_norm),
        grid=(T // rows,),
        in_specs=[
            pl.BlockSpec((rows, D), lambda i: (i, 0)),
            pl.BlockSpec((1, D, D), lambda i: (0, 0, 0), pipeline_mode=once),
            pl.BlockSpec((rows, D), lambda i: (i, 0)),
            pl.BlockSpec((1, D), lambda i: (0, 0)),
            pl.BlockSpec((1, D, D_FF), lambda i: (layer, 0, 0), pipeline_mode=once),
            pl.BlockSpec((1, D, D_FF), lambda i: (layer, 0, 1), pipeline_mode=once),
            pl.BlockSpec((1, D_FF, D), lambda i: (layer, 0, 0), pipeline_mode=once),
            pl.BlockSpec((1, D), lambda i: (0, 0)),
        ],
        out_specs=pl.BlockSpec((rows, D), lambda i: (i, 0)),
        out_shape=jax.ShapeDtypeStruct((T, D), F32),
        compiler_params=_params(1),
        name="proj_ffn_final" if final_norm else "proj_ffn",
    )(a, wp, res, norm_w, wgu, wgu, wd, final_w)


def _mlstm_proj_kernel(x_ref, nw_ref, wqt_ref, wk_ref, wvt_ref, wot_ref, wgt_ref, bg_ref,
                       qt_ref, k_ref, vt_ref, ogt_ref, gt_ref):
    xn = _rms(x_ref[0], nw_ref[...]).astype(BF16)
    scale = MLSTM_QK_DIM ** -0.5
    nt = lambda w_ref: lax.dot_general(w_ref[...], xn, _NT, preferred_element_type=F32)
    qt_ref[0] = (nt(wqt_ref) * scale).astype(BF16)
    k_ref[0] = jnp.dot(xn, wk_ref[...], preferred_element_type=F32).astype(BF16)
    vt_ref[0] = nt(wvt_ref).astype(BF16)
    ogt_ref[0] = jax.nn.sigmoid(nt(wot_ref)).astype(BF16)
    gates = nt(wgt_ref) + bg_ref[...]
    gt_ref[0] = GATE_SOFTCAP * jnp.tanh(gates / GATE_SOFTCAP)


def _mlstm_proj(x, norm_w, wqt, wk, wvt, wot, wgt, bg):
    B, S, D = x.shape
    rows = PROJ_ROWS
    H, dk = MLSTM_HEADS, MLSTM_QK_DIM
    const = lambda b, i: (0, 0)
    tok = lambda w: pl.BlockSpec((1, rows, w), lambda b, i: (b, i, 0))
    tok_t = lambda w: pl.BlockSpec((1, w, rows), lambda b, i: (b, 0, i))
    return pl.pallas_call(
        _mlstm_proj_kernel,
        grid=(B, S // rows),
        in_specs=[
            tok(D),
            pl.BlockSpec((1, D), const),
            pl.BlockSpec((H * dk, D), const),
            pl.BlockSpec((D, H * dk), const),
            pl.BlockSpec((D, D), const),
            pl.BlockSpec((D, D), const),
            pl.BlockSpec((2 * H, D), const),
            pl.BlockSpec((2 * H, 1), const),
        ],
        out_specs=[tok_t(H * dk), tok(H * dk), tok_t(D), tok_t(D), tok_t(2 * H)],
        out_shape=[
            jax.ShapeDtypeStruct((B, H * dk, S), BF16),
            jax.ShapeDtypeStruct((B, S, H * dk), BF16),
            jax.ShapeDtypeStruct((B, D, S), BF16),
            jax.ShapeDtypeStruct((B, D, S), BF16),
            jax.ShapeDtypeStruct((B, 2 * H, S), F32),
        ],
        compiler_params=_params(2),
        name="mlstm_proj",
    )(x, norm_w, wqt, wk, wvt, wot, wgt, bg)


def _lane_scan(x, op, fill):
    lane = lax.broadcasted_iota(jnp.int32, (x.shape[0], LANES), 1)
    groups, carry = [], None
    for t in range(x.shape[1] // LANES):
        y = x[:, t * LANES:(t + 1) * LANES]
        sh = 1
        while sh < LANES:
            y = op(y, jnp.where(lane >= sh, pltpu.roll(y, sh, 1), fill))
            sh *= 2
        if carry is not None:
            y = op(y, carry)
        carry = y[:, LANES - 1:LANES]
        groups.append(y)
    return jnp.concatenate(groups, axis=1)


def _mlstm_core_kernel(qt_ref, k_ref, vt_ref, ogt_ref, gt_ref, hw_ref, y_ref, c_sc, m_sc):
    L = MLSTM_CHUNK
    H, dk, dv = MLSTM_HEADS, MLSTM_QK_DIM, MLSTM_V_DIM

    @pl.when(pl.program_id(1) == 0)
    def _():
        c_sc[...] = jnp.zeros_like(c_sc)
        m_sc[...] = jnp.zeros_like(m_sc)

    s_idx = lax.broadcasted_iota(jnp.int32, (L, L), 0)
    l_idx = lax.broadcasted_iota(jnp.int32, (L, L), 1)
    causal = s_idx <= l_idx
    feat = lax.broadcasted_iota(jnp.int32, (LANES, L), 0)
    one_rows = jnp.where(lax.broadcasted_iota(jnp.int32, (16, L), 0) == 0, 1.0, 0.0).astype(BF16)

    for c in range(qt_ref.shape[2] // L):
        cols = slice(c * L, (c + 1) * L)
        gates = gt_ref[0, :, cols]
        ipre, fpre = gates[0:H], gates[H:2 * H]
        logf = jnp.minimum(fpre, 0.0) - jnp.log1p(jnp.exp(-jnp.abs(fpre)))
        b = _lane_scan(logf, jnp.add, 0.0)
        u = ipre - b
        m_prev = m_sc[:, 0:1]
        big_m = jnp.maximum(_lane_scan(u, jnp.maximum, NEG_INF), m_prev)
        inter = jnp.exp(m_prev - big_m)
        floor = jnp.exp(-(b + big_m))
        m_last = big_m[:, L - 1:L]
        w = jnp.exp(u - m_last)
        decay = jnp.exp(m_prev - m_last)
        m_sc[...] = jnp.broadcast_to(b[:, L - 1:L] + m_last, (H, LANES))
        u_cols = jnp.concatenate([u * LOG2E, jnp.zeros((LANES - H, L), F32)], axis=0).T
        big_m2 = big_m * LOG2E

        for hp in range(H // 2):
            k2 = k_ref[0, cols, hp * LANES:(hp + 1) * LANES]
            qt2 = qt_ref[0, hp * LANES:(hp + 1) * LANES, cols]
            for a in range(2):
                h = 2 * hp + a
                qta = jnp.where((feat >= a * dk) & (feat < (a + 1) * dk), qt2, jnp.zeros_like(qt2))
                s_t = jnp.dot(k2, qta, preferred_element_type=F32)
                e_t = jnp.exp2(jnp.where(causal, u_cols[:, h:h + 1] - big_m2[h:h + 1], NEG_INF))
                v_ext = jnp.concatenate([vt_ref[0, h * dv:(h + 1) * dv, cols], one_rows], axis=0)
                state = c_sc[h]
                r = (inter[h:h + 1] * jnp.dot(state.astype(BF16), qta, preferred_element_type=F32)
                     + jnp.dot(v_ext, (s_t * e_t).astype(BF16), preferred_element_type=F32))
                hh = r[:dv] / jnp.maximum(jnp.abs(r[dv:dv + 1]), floor[h:h + 1])
                hn = hh * lax.rsqrt(jnp.mean(hh * hh, axis=0, keepdims=True) + RMS_EPS)
                hw = hw_ref[h * dv:(h + 1) * dv, :]
                hn = hn * jnp.concatenate([hw] * (L // LANES), axis=1)
                y_t = ogt_ref[0, h * dv:(h + 1) * dv, cols].astype(F32) * hn
                y_ref[0, cols, h * dv:(h + 1) * dv] = y_t.T.astype(BF16)

                wv = (v_ext.astype(F32) * w[h:h + 1]).astype(BF16)
                c_sc[h] = decay[h:h + 1] * state + jnp.dot(wv, k2, preferred_element_type=F32)


def _mlstm_core(qt, k, vt, ogt, gt, head_w):
    B, D, S = vt.shape
    rows = MLSTM_ROWS
    H, dk, dv = MLSTM_HEADS, MLSTM_QK_DIM, MLSTM_V_DIM
    tok_t = lambda w: pl.BlockSpec((1, w, rows), lambda b, i: (b, 0, i))
    return pl.pallas_call(
        _mlstm_core_kernel,
        grid=(B, S // rows),
        in_specs=[
            tok_t(H * dk),
            pl.BlockSpec((1, rows, H * dk), lambda b, i: (b, i, 0)),
            tok_t(D), tok_t(D), tok_t(2 * H),
            pl.BlockSpec((D, LANES), lambda b, i: (0, 0)),
        ],
        out_specs=pl.BlockSpec((1, rows, D), lambda b, i: (b, i, 0)),
        out_shape=jax.ShapeDtypeStruct((B, S, D), BF16),
        scratch_shapes=[pltpu.VMEM((H, dv + 16, LANES), F32), pltpu.VMEM((H, LANES), F32)],
        compiler_params=_params(2),
        name="mlstm_core",
    )(qt, k, vt, ogt, gt, head_w)


def _rotary_tables(seq):
    pos = jnp.arange(seq, dtype=F32)
    inv_freq = ROPE_THETA ** (-jnp.arange(0, ROT_DIM, 2, dtype=F32) / ROT_DIM)
    ang = pos[:, None] * inv_freq[None, :]
    cos, sin = jnp.cos(ang), jnp.sin(ang)
    d = jnp.arange(LANES) % ATTN_HEAD_DIM
    cos_l, sin_l = cos[:, d % ROT_HALF], sin[:, d % ROT_HALF]
    cn = jnp.where(d < ROT_DIM, cos_l, 1.0)
    sa = jnp.where(d < ROT_HALF, -sin_l, 0.0)
    sb = jnp.where((d >= ROT_HALF) & (d < ROT_DIM), sin_l, 0.0)
    return cn, sa, sb, cos.T, sin.T


def kernel(x, attn_norm, attn_w_qkv, attn_w_o, mlstm_norm, mlstm_w_in, mlstm_b_gates, mlstm_head_norm,
           mlstm_w_out, ffn_norm, ffn_w_gate_up, ffn_w_down, final_norm):
    B, S, D = x.shape
    T = B * S
    H, dk = MLSTM_HEADS, MLSTM_QK_DIM
    row = lambda w: w.reshape(1, -1).astype(F32)

    cn, sa, sb, cost, sint = _rotary_tables(S)
    wqkv = attn_w_qkv[0].astype(BF16)
    qt, k, vt, kmean = _attn_proj(x, row(attn_norm[0]), wqkv[:, :D].T, wqkv[:, D:2 * D], wqkv[:, 2 * D:].T,
                                  cn, sa, sb, cost, sint)
    o = _moba(qt, k, vt, kmean.reshape(B, S // MOBA_BLOCK, D))
    wgu, wd = ffn_w_gate_up.astype(BF16), ffn_w_down.astype(BF16)
    h = _proj_ffn(o.reshape(T, D), attn_w_o.astype(BF16), x.reshape(T, D), row(ffn_norm[0]), wgu, wd,
                  row(final_norm), layer=0, final_norm=False)

    w_in = mlstm_w_in[0].astype(BF16)
    q_end, k_end, v_end, o_end = H * dk, 2 * H * dk, 2 * H * dk + D, 2 * H * dk + 2 * D
    qt, k, vt, ogt, gt = _mlstm_proj(
        h.reshape(B, S, D), row(mlstm_norm[0]), w_in[:, :q_end].T, w_in[:, q_end:k_end],
        w_in[:, k_end:v_end].T, w_in[:, v_end:o_end].T, w_in[:, o_end:].T,
        mlstm_b_gates[0].reshape(2 * H, 1).astype(F32))
    head_w = jnp.broadcast_to(mlstm_head_norm[0].astype(F32).reshape(D, 1), (D, LANES))
    y = _mlstm_core(qt, k, vt, ogt, gt, head_w)
    out = _proj_ffn(y.reshape(T, D), mlstm_w_out.astype(BF16), h, row(ffn_norm[1]), wgu, wd,
                    row(final_norm), layer=1, final_norm=True)
    return out.reshape(B, S, D)
```

```python
import functools

import jax
import jax.numpy as jnp
from jax import lax
from jax.experimental import pallas as pl
from jax.experimental.pallas import tpu as pltpu

F32 = jnp.float32
BF16 = jnp.bfloat16

D_MODEL = 1024
ATTN_HEADS = 16
ATTN_HEAD_DIM = 64
ROT_DIM = 16
ROT_HALF = ROT_DIM // 2
ROPE_THETA = 500000.0
MOBA_BLOCK = 256
MOBA_TOPK = 3
MLSTM_HEADS = 8
MLSTM_QK_DIM = 64
MLSTM_V_DIM = 128
GATE_SOFTCAP = 15.0
D_FF = 2816
RMS_EPS = 1e-6

LANES = 128
VMEM_LIMIT_BYTES = 56 * 1024 * 1024
PROJ_ROWS = 512
FFN_ROWS = 512
MLSTM_CHUNK = 256
MLSTM_ROWS = 512
MOBA_HEADS = 4
MOBA_STEP = 6
LOG2E = 1.4426950408889634
NEG_INF = float("-inf")

_NT = (((1,), (1,)), ((), ()))


def _params(n_grid):
    return pltpu.CompilerParams(dimension_semantics=("arbitrary",) * n_grid,
                                vmem_limit_bytes=VMEM_LIMIT_BYTES)


def _rms(x, w):
    return x * lax.rsqrt(jnp.mean(x * x, axis=-1, keepdims=True) + RMS_EPS) * w


def _attn_proj_kernel(x_ref, nw_ref, wqt_ref, wk_ref, wvt_ref, cn_ref, sa_ref, sb_ref, cost_ref,
                      sint_ref, qt_ref, k_ref, vt_ref, kmean_ref):
    rows = x_ref.shape[1]
    xn = _rms(x_ref[0], nw_ref[...]).astype(BF16)

    k = jnp.dot(xn, wk_ref[...], preferred_element_type=F32)
    cn, sa, sb = cn_ref[...], sa_ref[...], sb_ref[...]
    for g in range(D_MODEL // LANES):
        kg = k[:, g * LANES:(g + 1) * LANES]
        kr = (kg * cn + pltpu.roll(kg, LANES - ROT_HALF, 1) * sa + pltpu.roll(kg, ROT_HALF, 1) * sb)
        k_ref[0, :, g * LANES:(g + 1) * LANES] = kr.astype(BF16)
        for t in range(rows // MOBA_BLOCK):
            kmean_ref[0, 0, t:t + 1, g * LANES:(g + 1) * LANES] = jnp.mean(
                kr[t * MOBA_BLOCK:(t + 1) * MOBA_BLOCK], axis=0, keepdims=True)

    qt = lax.dot_general(wqt_ref[...], xn, _NT, preferred_element_type=F32)
    ct, st = cost_ref[...], sint_ref[...]
    scale = ATTN_HEAD_DIM ** -0.5 * LOG2E
    for h in range(ATTN_HEADS):
        blk = qt[h * ATTN_HEAD_DIM:(h + 1) * ATTN_HEAD_DIM]
        x1, x2 = blk[0:ROT_HALF], blk[ROT_HALF:ROT_DIM]
        r = jnp.concatenate([x1 * ct - x2 * st, x2 * ct + x1 * st, blk[ROT_DIM:]], axis=0) * scale
        qt_ref[0, h * ATTN_HEAD_DIM:(h + 1) * ATTN_HEAD_DIM, :] = r.astype(BF16)

    vt = lax.dot_general(wvt_ref[...], xn, _NT, preferred_element_type=F32)
    for t in range(rows // MOBA_BLOCK):
        vt_ref[0, t] = vt[:, t * MOBA_BLOCK:(t + 1) * MOBA_BLOCK].astype(BF16)


def _attn_proj(x, norm_w, wqt, wk, wvt, cn, sa, sb, cost, sint):
    B, S, D = x.shape
    rows = PROJ_ROWS
    nb_step = rows // MOBA_BLOCK
    const = lambda b, i: (0, 0)
    return pl.pallas_call(
        _attn_proj_kernel,
        grid=(B, S // rows),
        in_specs=[
            pl.BlockSpec((1, rows, D), lambda b, i: (b, i, 0)),
            pl.BlockSpec((1, D), const),
            pl.BlockSpec((D, D), const),
            pl.BlockSpec((D, D), const),
            pl.BlockSpec((D, D), const),
            pl.BlockSpec((rows, LANES), lambda b, i: (i, 0)),
            pl.BlockSpec((rows, LANES), lambda b, i: (i, 0)),
            pl.BlockSpec((rows, LANES), lambda b, i: (i, 0)),
            pl.BlockSpec((ROT_HALF, rows), lambda b, i: (0, i)),
            pl.BlockSpec((ROT_HALF, rows), lambda b, i: (0, i)),
        ],
        out_specs=[
            pl.BlockSpec((1, D, rows), lambda b, i: (b, 0, i)),
            pl.BlockSpec((1, rows, D), lambda b, i: (b, i, 0)),
            pl.BlockSpec((1, nb_step, D, MOBA_BLOCK), lambda b, i: (b, i, 0, 0)),
            pl.BlockSpec((1, 1, nb_step, D), lambda b, i: (b, i, 0, 0)),
        ],
        out_shape=[
            jax.ShapeDtypeStruct((B, D, S), BF16),
            jax.ShapeDtypeStruct((B, S, D), BF16),
            jax.ShapeDtypeStruct((B, S // MOBA_BLOCK, D, MOBA_BLOCK), BF16),
            jax.ShapeDtypeStruct((B, S // rows, nb_step, D), F32),
        ],
        compiler_params=_params(2),
        name="attn_proj",
    )(x, norm_w, wqt, wk, wvt, cn, sa, sb, cost, sint)


def _moba_kernel(qt_ref, k_ref, vt_ref, km_ref, o_ref, sel_ref, s_sc):
    cur = pl.program_id(2)
    nblk = km_ref.shape[1]
    blk_q = qt_ref.shape[2]
    hd = ATTN_HEAD_DIM
    heads = range(MOBA_HEADS)
    row = lax.broadcasted_iota(jnp.int32, (LANES, blk_q), 0)
    blk = lax.broadcasted_iota(jnp.int32, (nblk, blk_q), 0)
    kpos = lax.broadcasted_iota(jnp.int32, (MOBA_BLOCK, blk_q), 0)
    qpos = lax.broadcasted_iota(jnp.int32, (MOBA_BLOCK, blk_q), 1)
    ones_rows = jnp.ones((16, MOBA_BLOCK), BF16)

    qpads = []
    for a in heads:
        g, r = a // 2, a % 2
        qt2 = qt_ref[0, g * LANES:(g + 1) * LANES, :]
        qta = jnp.where((row >= r * hd) & (row < (r + 1) * hd), qt2, jnp.zeros_like(qt2))
        qpads.append(qta)
        km = km_ref[0, :, g * LANES:(g + 1) * LANES].astype(BF16)
        gate = jnp.dot(km, qta, preferred_element_type=F32)
        gate = jnp.where(blk < cur, gate, NEG_INF)
        sel = jnp.zeros((nblk, blk_q), F32)
        for _ in range(MOBA_TOPK):
            mx = jnp.max(gate, axis=0, keepdims=True)
            idx = jnp.min(jnp.where(gate == mx, blk, nblk), axis=0, keepdims=True)
            hit = blk == idx
            sel = jnp.where(hit & (blk < cur), 1.0, sel)
            gate = jnp.where(hit, NEG_INF, gate)
        sel_ref[a] = sel

    def scores(j):
        jj = jnp.minimum(j, nblk - 1)
        rows = pl.ds(pl.multiple_of(jj * MOBA_BLOCK, MOBA_BLOCK), MOBA_BLOCK)
        return [jnp.dot(k_ref[0, rows, (a // 2) * LANES:(a // 2 + 1) * LANES], qpads[a],
                        preferred_element_type=F32) for a in heads]

    def v_ext(j, a):
        return jnp.concatenate([vt_ref[0, j, a * hd:(a + 1) * hd, :], ones_rows], axis=0)

    m_acc = []
    for a, s in enumerate(scores(cur)):
        s = jnp.where(kpos <= qpos, s, NEG_INF)
        m = jnp.max(s, axis=0, keepdims=True)
        p = jnp.exp2(s - m).astype(BF16)
        m_acc += [m, jnp.dot(v_ext(cur, a), p, preferred_element_type=F32)]

    def park(s, slot):
        for a in heads:
            s_sc[slot, a] = s[a]
        return [jnp.max(s[a], axis=0, keepdims=True) for a in heads]

    def consume(j, slot, smax, m_acc):
        jj = jnp.minimum(j, nblk - 1)
        out = []
        for a in heads:
            m, acc = m_acc[2 * a], m_acc[2 * a + 1]
            chosen = sel_ref[a, pl.ds(jj, 1), :] > 0.5
            m_new = jnp.where(chosen, jnp.maximum(m, smax[a]), m)
            p = jnp.exp2(s_sc[slot, a] - jnp.where(chosen, m_new, jnp.inf)).astype(BF16)
            acc = jnp.exp2(m - m_new) * acc + jnp.dot(v_ext(jj, a), p, preferred_element_type=F32)
            out += [m_new, acc]
        return out

    nh = MOBA_HEADS

    def steps(n, base):
        def body(i, carry):
            m_acc = list(carry[:2 * nh])
            smax = {0: list(carry[2 * nh:3 * nh]), 1: list(carry[3 * nh:])}
            j = base + n * i
            for t in range(n):
                s_next = scores(j + t + 2)
                m_acc = consume(j + t, t, smax[t], m_acc)
                smax[(t + 2) % n] = park(s_next, (t + 2) % n)
            return tuple(m_acc + smax[0] + smax[1])
        return body

    init = tuple(m_acc + park(scores(0), 0) + park(scores(1), 1))
    full = cur // MOBA_STEP
    carry = lax.fori_loop(0, full, steps(MOBA_STEP, 0), init)
    left = cur - MOBA_STEP * full
    carry = lax.fori_loop(0, (left + 1) // 2, steps(2, MOBA_STEP * full), carry)
    outs = [carry[2 * a + 1][:hd] / carry[2 * a + 1][hd:hd + 1] for a in heads]
    o_ref[0] = jnp.concatenate(outs, axis=0).T.astype(BF16)


def _moba(qt, k, vt, kmean):
    B, D, S = qt.shape
    nblk = S // MOBA_BLOCK
    width = MOBA_HEADS * ATTN_HEAD_DIM
    return pl.pallas_call(
        _moba_kernel,
        grid=(B, D // width, nblk),
        in_specs=[
            pl.BlockSpec((1, width, MOBA_BLOCK), lambda b, p, i: (b, p, i)),
            pl.BlockSpec((1, S, width), lambda b, p, i: (b, 0, p)),
            pl.BlockSpec((1, nblk, width, MOBA_BLOCK), lambda b, p, i: (b, 0, p, 0)),
            pl.BlockSpec((1, nblk, width), lambda b, p, i: (b, 0, p)),
        ],
        out_specs=pl.BlockSpec((1, MOBA_BLOCK, width), lambda b, p, i: (b, i, p)),
        out_shape=jax.ShapeDtypeStruct((B, S, D), BF16),
        scratch_shapes=[pltpu.VMEM((MOBA_HEADS, nblk, MOBA_BLOCK), F32),
                        pltpu.VMEM((MOBA_STEP, MOBA_HEADS, MOBA_BLOCK, MOBA_BLOCK), F32)],
        compiler_params=_params(3),
        name="moba",
    )(qt, k, vt, kmean)


def _proj_ffn_kernel(a_ref, wp_ref, res_ref, nw_ref, wg_ref, wu_ref, wd_ref, fw_ref, o_ref, *, final_norm):
    h = jnp.dot(a_ref[...], wp_ref[0], preferred_element_type=F32) + res_ref[...]
    xn = _rms(h, nw_ref[...]).astype(BF16)
    g = jnp.dot(xn, wg_ref[0], preferred_element_type=F32)
    u = jnp.dot(xn, wu_ref[0], preferred_element_type=F32)
    act = (g * jax.nn.sigmoid(g) * u).astype(BF16)
    h = h + jnp.dot(act, wd_ref[0], preferred_element_type=F32)
    o_ref[...] = _rms(h, fw_ref[...]) if final_norm else h


def _proj_ffn(a, wp, res, norm_w, wgu, wd, final_w, *, layer, final_norm):
    T, D = res.shape
    rows = FFN_ROWS
    once = pl.Buffered(1)
    return pl.pallas_call(
        functools.partial(_proj_ffn_kernel, final_norm=final_norm),
        grid=(T // rows,),
        in_specs=[
            pl.BlockSpec((rows, D), lambda i: (i, 0)),
            pl.BlockSpec((1, D, D), lambda i: (0, 0, 0), pipeline_mode=once),
            pl.BlockSpec((rows, D), lambda i: (i, 0)),
            pl.BlockSpec((1, D), lambda i: (0, 0)),
            pl.BlockSpec((1, D, D_FF), lambda i: (layer, 0, 0), pipeline_mode=once),
            pl.BlockSpec((1, D, D_FF), lambda i: (layer, 0, 1), pipeline_mode=once),
            pl.BlockSpec((1, D_FF, D), lambda i: (layer, 0, 0), pipeline_mode=once),
            pl.BlockSpec((1, D), lambda i: (0, 0)),
        ],
        out_specs=pl.BlockSpec((rows, D), lambda i: (i, 0)),
        out_shape=jax.ShapeDtypeStruct((T, D), F32),
        compiler_params=_params(1),
        name="proj_ffn_final" if final_norm else "proj_ffn",
    )(a, wp, res, norm_w, wgu, wgu, wd, final_w)


def _mlstm_proj_kernel(x_ref, nw_ref, wqt_ref, wk_ref, wvt_ref, wot_ref, wgt_ref, bg_ref,
                       qt_ref, k_ref, vt_ref, ogt_ref, gt_ref):
    xn = _rms(x_ref[0], nw_ref[...]).astype(BF16)
    scale = MLSTM_QK_DIM ** -0.5
    nt = lambda w_ref: lax.dot_general(w_ref[...], xn, _NT, preferred_element_type=F32)
    qt_ref[0] = (nt(wqt_ref) * scale).astype(BF16)
    k_ref[0] = jnp.dot(xn, wk_ref[...], preferred_element_type=F32).astype(BF16)
    vt_ref[0] = nt(wvt_ref).astype(BF16)
    ogt_ref[0] = jax.nn.sigmoid(nt(wot_ref)).astype(BF16)
    gates = nt(wgt_ref) + bg_ref[...]
    gt_ref[0] = GATE_SOFTCAP * jnp.tanh(gates / GATE_SOFTCAP)


def _mlstm_proj(x, norm_w, wqt, wk, wvt, wot, wgt, bg):
    B, S, D = x.shape
    rows = PROJ_ROWS
    H, dk = MLSTM_HEADS, MLSTM_QK_DIM
    const = lambda b, i: (0, 0)
    tok = lambda w: pl.BlockSpec((1, rows, w), lambda b, i: (b, i, 0))
    tok_t = lambda w: pl.BlockSpec((1, w, rows), lambda b, i: (b, 0, i))
    return pl.pallas_call(
        _mlstm_proj_kernel,
        grid=(B, S // rows),
        in_specs=[
            tok(D),
            pl.BlockSpec((1, D), const),
            pl.BlockSpec((H * dk, D), const),
            pl.BlockSpec((D, H * dk), const),
            pl.BlockSpec((D, D), const),
            pl.BlockSpec((D, D), const),
            pl.BlockSpec((2 * H, D), const),
            pl.BlockSpec((2 * H, 1), const),
        ],
        out_specs=[tok_t(H * dk), tok(H * dk), tok_t(D), tok_t(D), tok_t(2 * H)],
        out_shape=[
            jax.ShapeDtypeStruct((B, H * dk, S), BF16),
            jax.ShapeDtypeStruct((B, S, H * dk), BF16),
            jax.ShapeDtypeStruct((B, D, S), BF16),
            jax.ShapeDtypeStruct((B, D, S), BF16),
            jax.ShapeDtypeStruct((B, 2 * H, S), F32),
        ],
        compiler_params=_params(2),
        name="mlstm_proj",
    )(x, norm_w, wqt, wk, wvt, wot, wgt, bg)


def _lane_scan(x, op, fill):
    lane = lax.broadcasted_iota(jnp.int32, (x.shape[0], LANES), 1)
    groups, carry = [], None
    for t in range(x.shape[1] // LANES):
        y = x[:, t * LANES:(t + 1) * LANES]
        sh = 1
        while sh < LANES:
            y = op(y, jnp.where(lane >= sh, pltpu.roll(y, sh, 1), fill))
            sh *= 2
        if carry is not None:
            y = op(y, carry)
        carry = y[:, LANES - 1:LANES]
        groups.append(y)
    return jnp.concatenate(groups, axis=1)


def _mlstm_core_kernel(qt_ref, k_ref, vt_ref, ogt_ref, gt_ref, hw_ref, y_ref, c_sc, m_sc):
    L = MLSTM_CHUNK
    H, dk, dv = MLSTM_HEADS, MLSTM_QK_DIM, MLSTM_V_DIM

    @pl.when(pl.program_id(1) == 0)
    def _():
        c_sc[...] = jnp.zeros_like(c_sc)
        m_sc[...] = jnp.zeros_like(m_sc)

    s_idx = lax.broadcasted_iota(jnp.int32, (L, L), 0)
    l_idx = lax.broadcasted_iota(jnp.int32, (L, L), 1)
    causal = s_idx <= l_idx
    feat = lax.broadcasted_iota(jnp.int32, (LANES, L), 0)
    one_rows = jnp.where(lax.broadcasted_iota(jnp.int32, (16, L), 0) == 0, 1.0, 0.0).astype(BF16)

    for c in range(qt_ref.shape[2] // L):
        cols = slice(c * L, (c + 1) * L)
        gates = gt_ref[0, :, cols]
        ipre, fpre = gates[0:H], gates[H:2 * H]
        logf = jnp.minimum(fpre, 0.0) - jnp.log1p(jnp.exp(-jnp.abs(fpre)))
        b = _lane_scan(logf, jnp.add, 0.0)
        u = ipre - b
        m_prev = m_sc[:, 0:1]
        big_m = jnp.maximum(_lane_scan(u, jnp.maximum, NEG_INF), m_prev)
        inter = jnp.exp(m_prev - big_m)
        floor = jnp.exp(-(b + big_m))
        m_last = big_m[:, L - 1:L]
        w = jnp.exp(u - m_last)
        decay = jnp.exp(m_prev - m_last)
        m_sc[...] = jnp.broadcast_to(b[:, L - 1:L] + m_last, (H, LANES))
        u_cols = jnp.concatenate([u * LOG2E, jnp.zeros((LANES - H, L), F32)], axis=0).T
        big_m2 = big_m * LOG2E

        for hp in range(H // 2):
            k2 = k_ref[0, cols, hp * LANES:(hp + 1) * LANES]
            qt2 = qt_ref[0, hp * LANES:(hp + 1) * LANES, cols]
            for a in range(2):
                h = 2 * hp + a
                qta = jnp.where((feat >= a * dk) & (feat < (a + 1) * dk), qt2, jnp.zeros_like(qt2))
                s_t = jnp.dot(k2, qta, preferred_element_type=F32)
                e_t = jnp.exp2(jnp.where(causal, u_cols[:, h:h + 1] - big_m2[h:h + 1], NEG_INF))
                v_ext = jnp.concatenate([vt_ref[0, h * dv:(h + 1) * dv, cols], one_rows], axis=0)
                state = c_sc[h]
                r = (inter[h:h + 1] * jnp.dot(state.astype(BF16), qta, preferred_element_type=F32)
                     + jnp.dot(v_ext, (s_t * e_t).astype(BF16), preferred_element_type=F32))
                hh = r[:dv] / jnp.maximum(jnp.abs(r[dv:dv + 1]), floor[h:h + 1])
                hn = hh * lax.rsqrt(jnp.mean(hh * hh, axis=0, keepdims=True) + RMS_EPS)
                hw = hw_ref[h * dv:(h + 1) * dv, :]
                hn = hn * jnp.concatenate([hw] * (L // LANES), axis=1)
                y_t = ogt_ref[0, h * dv:(h + 1) * dv, cols].astype(F32) * hn
                y_ref[0, cols, h * dv:(h + 1) * dv] = y_t.T.astype(BF16)

                wv = (v_ext.astype(F32) * w[h:h + 1]).astype(BF16)
                c_sc[h] = decay[h:h + 1] * state + jnp.dot(wv, k2, preferred_element_type=F32)


def _mlstm_core(qt, k, vt, ogt, gt, head_w):
    B, D, S = vt.shape
    rows = MLSTM_ROWS
    H, dk, dv = MLSTM_HEADS, MLSTM_QK_DIM, MLSTM_V_DIM
    tok_t = lambda w: pl.BlockSpec((1, w, rows), lambda b, i: (b, 0, i))
    return pl.pallas_call(
        _mlstm_core_kernel,
        grid=(B, S // rows),
        in_specs=[
            tok_t(H * dk),
            pl.BlockSpec((1, rows, H * dk), lambda b, i: (b, i, 0)),
            tok_t(D), tok_t(D), tok_t(2 * H),
            pl.BlockSpec((D, LANES), lambda b, i: (0, 0)),
        ],
        out_specs=pl.BlockSpec((1, rows, D), lambda b, i: (b, i, 0)),
        out_shape=jax.ShapeDtypeStruct((B, S, D), BF16),
        scratch_shapes=[pltpu.VMEM((H, dv + 16, LANES), F32), pltpu.VMEM((H, LANES), F32)],
        compiler_params=_params(2),
        name="mlstm_core",
    )(qt, k, vt, ogt, gt, head_w)


def _rotary_tables(seq):
    pos = jnp.arange(seq, dtype=F32)
    inv_freq = ROPE_THETA ** (-jnp.arange(0, ROT_DIM, 2, dtype=F32) / ROT_DIM)
    ang = pos[:, None] * inv_freq[None, :]
    cos, sin = jnp.cos(ang), jnp.sin(ang)
    d = jnp.arange(LANES) % ATTN_HEAD_DIM
    cos_l, sin_l = cos[:, d % ROT_HALF], sin[:, d % ROT_HALF]
    cn = jnp.where(d < ROT_DIM, cos_l, 1.0)
    sa = jnp.where(d < ROT_HALF, -sin_l, 0.0)
    sb = jnp.where((d >= ROT_HALF) & (d < ROT_DIM), sin_l, 0.0)
    return cn, sa, sb, cos.T, sin.T


def kernel(x, attn_norm, attn_w_qkv, attn_w_o, mlstm_norm, mlstm_w_in, mlstm_b_gates, mlstm_head_norm,
           mlstm_w_out, ffn_norm, ffn_w_gate_up, ffn_w_down, final_norm):
    B, S, D = x.shape
    T = B * S
    H, dk = MLSTM_HEADS, MLSTM_QK_DIM
    row = lambda w: w.reshape(1, -1).astype(F32)

    cn, sa, sb, cost, sint = _rotary_tables(S)
    wqkv = attn_w_qkv[0].astype(BF16)
    qt, k, vt, kmean = _attn_proj(x, row(attn_norm[0]), wqkv[:, :D].T, wqkv[:, D:2 * D], wqkv[:, 2 * D:].T,
                                  cn, sa, sb, cost, sint)
    o = _moba(qt, k, vt, kmean.reshape(B, S // MOBA_BLOCK, D))
    wgu, wd = ffn_w_gate_up.astype(BF16), ffn_w_down.astype(BF16)
    h = _proj_ffn(o.reshape(T, D), attn_w_o.astype(BF16), x.reshape(T, D), row(ffn_norm[0]), wgu, wd,
                  row(final_norm), layer=0, final_norm=False)

    w_in = mlstm_w_in[0].astype(BF16)
    q_end, k_end, v_end, o_end = H * dk, 2 * H * dk, 2 * H * dk + D, 2 * H * dk + 2 * D
    qt, k, vt, ogt, gt = _mlstm_proj(
        h.reshape(B, S, D), row(mlstm_norm[0]), w_in[:, :q_end].T, w_in[:, q_end:k_end],
        w_in[:, k_end:v_end].T, w_in[:, v_end:o_end].T, w_in[:, o_end:].T,
        mlstm_b_gates[0].reshape(2 * H, 1).astype(F32))
    head_w = jnp.broadcast_to(mlstm_head_norm[0].astype(F32).reshape(D, 1), (D, LANES))
    y = _mlstm_core(qt, k, vt, ogt, gt, head_w)
    out = _proj_ffn(y.reshape(T, D), mlstm_w_out.astype(BF16), h, row(ffn_norm[1]), wgu, wd,
                    row(final_norm), layer=1, final_norm=True)
    return out.reshape(B, S, D)
```

```python
import functools

import jax
import jax.numpy as jnp
from jax import lax
from jax.experimental import pallas as pl
from jax.experimental.pallas import tpu as pltpu

F32 = jnp.float32
BF16 = jnp.bfloat16

D_MODEL = 1024
ATTN_HEADS = 16
ATTN_HEAD_DIM = 64
ROT_DIM = 16
ROT_HALF = ROT_DIM // 2
ROPE_THETA = 500000.0
MOBA_BLOCK = 256
MOBA_TOPK = 3
MLSTM_HEADS = 8
MLSTM_QK_DIM = 64
MLSTM_V_DIM = 128
GATE_SOFTCAP = 15.0
D_FF = 2816
RMS_EPS = 1e-6

LANES = 128
VMEM_LIMIT_BYTES = 56 * 1024 * 1024
PROJ_ROWS = 512
FFN_ROWS = 512
MLSTM_CHUNK = 256
MLSTM_ROWS = 512
MOBA_HEADS = 4
MOBA_STEP = 8
LOG2E = 1.4426950408889634
NEG_INF = float("-inf")

_NT = (((1,), (1,)), ((), ()))


def _params(n_grid):
    return pltpu.CompilerParams(dimension_semantics=("arbitrary",) * n_grid,
                                vmem_limit_bytes=VMEM_LIMIT_BYTES)


def _rms(x, w):
    return x * lax.rsqrt(jnp.mean(x * x, axis=-1, keepdims=True) + RMS_EPS) * w


def _attn_proj_kernel(x_ref, nw_ref, wqt_ref, wk_ref, wvt_ref, cn_ref, sa_ref, sb_ref, cost_ref,
                      sint_ref, qt_ref, k_ref, vt_ref, kmean_ref):
    rows = x_ref.shape[1]
    xn = _rms(x_ref[0], nw_ref[...]).astype(BF16)

    k = jnp.dot(xn, wk_ref[...], preferred_element_type=F32)
    cn, sa, sb = cn_ref[...], sa_ref[...], sb_ref[...]
    for g in range(D_MODEL // LANES):
        kg = k[:, g * LANES:(g + 1) * LANES]
        kr = (kg * cn + pltpu.roll(kg, LANES - ROT_HALF, 1) * sa + pltpu.roll(kg, ROT_HALF, 1) * sb)
        k_ref[0, :, g * LANES:(g + 1) * LANES] = kr.astype(BF16)
        for t in range(rows // MOBA_BLOCK):
            kmean_ref[0, 0, t:t + 1, g * LANES:(g + 1) * LANES] = jnp.mean(
                kr[t * MOBA_BLOCK:(t + 1) * MOBA_BLOCK], axis=0, keepdims=True)

    qt = lax.dot_general(wqt_ref[...], xn, _NT, preferred_element_type=F32)
    ct, st = cost_ref[...], sint_ref[...]
    scale = ATTN_HEAD_DIM ** -0.5 * LOG2E
    for h in range(ATTN_HEADS):
        blk = qt[h * ATTN_HEAD_DIM:(h + 1) * ATTN_HEAD_DIM]
        x1, x2 = blk[0:ROT_HALF], blk[ROT_HALF:ROT_DIM]
        r = jnp.concatenate([x1 * ct - x2 * st, x2 * ct + x1 * st, blk[ROT_DIM:]], axis=0) * scale
        qt_ref[0, h * ATTN_HEAD_DIM:(h + 1) * ATTN_HEAD_DIM, :] = r.astype(BF16)

    vt = lax.dot_general(wvt_ref[...], xn, _NT, preferred_element_type=F32)
    for t in range(rows // MOBA_BLOCK):
        vt_ref[0, t] = vt[:, t * MOBA_BLOCK:(t + 1) * MOBA_BLOCK].astype(BF16)


def _attn_proj(x, norm_w, wqt, wk, wvt, cn, sa, sb, cost, sint):
    B, S, D = x.shape
    rows = PROJ_ROWS
    nb_step = rows // MOBA_BLOCK
    const = lambda b, i: (0, 0)
    return pl.pallas_call(
        _attn_proj_kernel,
        grid=(B, S // rows),
        in_specs=[
            pl.BlockSpec((1, rows, D), lambda b, i: (b, i, 0)),
            pl.BlockSpec((1, D), const),
            pl.BlockSpec((D, D), const),
            pl.BlockSpec((D, D), const),
            pl.BlockSpec((D, D), const),
            pl.BlockSpec((rows, LANES), lambda b, i: (i, 0)),
            pl.BlockSpec((rows, LANES), lambda b, i: (i, 0)),
            pl.BlockSpec((rows, LANES), lambda b, i: (i, 0)),
            pl.BlockSpec((ROT_HALF, rows), lambda b, i: (0, i)),
            pl.BlockSpec((ROT_HALF, rows), lambda b, i: (0, i)),
        ],
        out_specs=[
            pl.BlockSpec((1, D, rows), lambda b, i: (b, 0, i)),
            pl.BlockSpec((1, rows, D), lambda b, i: (b, i, 0)),
            pl.BlockSpec((1, nb_step, D, MOBA_BLOCK), lambda b, i: (b, i, 0, 0)),
            pl.BlockSpec((1, 1, nb_step, D), lambda b, i: (b, i, 0, 0)),
        ],
        out_shape=[
            jax.ShapeDtypeStruct((B, D, S), BF16),
            jax.ShapeDtypeStruct((B, S, D), BF16),
            jax.ShapeDtypeStruct((B, S // MOBA_BLOCK, D, MOBA_BLOCK), BF16),
            jax.ShapeDtypeStruct((B, S // rows, nb_step, D), F32),
        ],
        compiler_params=_params(2),
        name="attn_proj",
    )(x, norm_w, wqt, wk, wvt, cn, sa, sb, cost, sint)


def _moba_kernel(qt_ref, k_ref, vt_ref, km_ref, o_ref, sel_ref, s_sc):
    cur = pl.program_id(2)
    nblk = km_ref.shape[1]
    blk_q = qt_ref.shape[2]
    hd = ATTN_HEAD_DIM
    heads = range(MOBA_HEADS)
    row = lax.broadcasted_iota(jnp.int32, (LANES, blk_q), 0)
    blk = lax.broadcasted_iota(jnp.int32, (nblk, blk_q), 0)
    kpos = lax.broadcasted_iota(jnp.int32, (MOBA_BLOCK, blk_q), 0)
    qpos = lax.broadcasted_iota(jnp.int32, (MOBA_BLOCK, blk_q), 1)
    ones_rows = jnp.ones((16, MOBA_BLOCK), BF16)

    qpads = []
    for a in heads:
        g, r = a // 2, a % 2
        qt2 = qt_ref[0, g * LANES:(g + 1) * LANES, :]
        qta = jnp.where((row >= r * hd) & (row < (r + 1) * hd), qt2, jnp.zeros_like(qt2))
        qpads.append(qta)
        km = km_ref[0, :, g * LANES:(g + 1) * LANES].astype(BF16)
        gate = jnp.dot(km, qta, preferred_element_type=F32)
        gate = jnp.where(blk < cur, gate, NEG_INF)
        sel = jnp.zeros((nblk, blk_q), F32)
        for _ in range(MOBA_TOPK):
            mx = jnp.max(gate, axis=0, keepdims=True)
            idx = jnp.min(jnp.where(gate == mx, blk, nblk), axis=0, keepdims=True)
            hit = blk == idx
            sel = jnp.where(hit & (blk < cur), 1.0, sel)
            gate = jnp.where(hit, NEG_INF, gate)
        sel_ref[a] = sel

    def scores(j):
        jj = jnp.minimum(j, nblk - 1)
        rows = pl.ds(pl.multiple_of(jj * MOBA_BLOCK, MOBA_BLOCK), MOBA_BLOCK)
        return [jnp.dot(k_ref[0, rows, (a // 2) * LANES:(a // 2 + 1) * LANES], qpads[a],
                        preferred_element_type=F32) for a in heads]

    def v_ext(j, a):
        return jnp.concatenate([vt_ref[0, j, a * hd:(a + 1) * hd, :], ones_rows], axis=0)

    m_acc = []
    for a, s in enumerate(scores(cur)):
        s = jnp.where(kpos <= qpos, s, NEG_INF)
        m = jnp.max(s, axis=0, keepdims=True)
        p = jnp.exp2(s - m).astype(BF16)
        m_acc += [m, jnp.dot(v_ext(cur, a), p, preferred_element_type=F32)]

    def park(s, slot):
        for a in heads:
            s_sc[slot, a] = s[a]
        return [jnp.max(s[a], axis=0, keepdims=True) for a in heads]

    def consume(j, slot, smax, m_acc):
        jj = jnp.minimum(j, nblk - 1)
        out = []
        for a in heads:
            m, acc = m_acc[2 * a], m_acc[2 * a + 1]
            chosen = sel_ref[a, pl.ds(jj, 1), :] > 0.5
            m_new = jnp.where(chosen, jnp.maximum(m, smax[a]), m)
            p = jnp.exp2(s_sc[slot, a] - jnp.where(chosen, m_new, jnp.inf)).astype(BF16)
            acc = jnp.exp2(m - m_new) * acc + jnp.dot(v_ext(jj, a), p, preferred_element_type=F32)
            out += [m_new, acc]
        return out

    nh = MOBA_HEADS

    def steps(n, base):
        def body(i, carry):
            m_acc = list(carry[:2 * nh])
            smax = {0: list(carry[2 * nh:3 * nh]), 1: list(carry[3 * nh:])}
            j = base + n * i
            for t in range(n):
                s_next = scores(j + t + 2)
                m_acc = consume(j + t, t, smax[t], m_acc)
                smax[(t + 2) % n] = park(s_next, (t + 2) % n)
            return tuple(m_acc + smax[0] + smax[1])
        return body

    init = tuple(m_acc + park(scores(0), 0) + park(scores(1), 1))
    full = cur // MOBA_STEP
    carry = lax.fori_loop(0, full, steps(MOBA_STEP, 0), init)
    left = cur - MOBA_STEP * full
    carry = lax.fori_loop(0, (left + 1) // 2, steps(2, MOBA_STEP * full), carry)
    outs = [carry[2 * a + 1][:hd] / carry[2 * a + 1][hd:hd + 1] for a in heads]
    o_ref[0] = jnp.concatenate(outs, axis=0).T.astype(BF16)


def _moba(qt, k, vt, kmean):
    B, D, S = qt.shape
    nblk = S // MOBA_BLOCK
    width = MOBA_HEADS * ATTN_HEAD_DIM
    return pl.pallas_call(
        _moba_kernel,
        grid=(B, D // width, nblk),
        in_specs=[
            pl.BlockSpec((1, width, MOBA_BLOCK), lambda b, p, i: (b, p, i)),
            pl.BlockSpec((1, S, width), lambda b, p, i: (b, 0, p)),
            pl.BlockSpec((1, nblk, width, MOBA_BLOCK), lambda b, p, i: (b, 0, p, 0)),
            pl.BlockSpec((1, nblk, width), lambda b, p, i: (b, 0, p)),
        ],
        out_specs=pl.BlockSpec((1, MOBA_BLOCK, width), lambda b, p, i: (b, i, p)),
        out_shape=jax.ShapeDtypeStruct((B, S, D), BF16),
        scratch_shapes=[pltpu.VMEM((MOBA_HEADS, nblk, MOBA_BLOCK), F32),
                        pltpu.VMEM((MOBA_STEP, MOBA_HEADS, MOBA_BLOCK, MOBA_BLOCK), F32)],
        compiler_params=_params(3),
        name="moba",
    )(qt, k, vt, kmean)


def _proj_ffn_kernel(a_ref, wp_ref, res_ref, nw_ref, wg_ref, wu_ref, wd_ref, fw_ref, o_ref, *, final_norm):
    h = jnp.dot(a_ref[...], wp_ref[0], preferred_element_type=F32) + res_ref[...]
    xn = _rms(h, nw_ref[...]).astype(BF16)
    g = jnp.dot(xn, wg_ref[0], preferred_element_type=F32)
    u = jnp.dot(xn, wu_ref[0], preferred_element_type=F32)
    act = (g * jax.nn.sigmoid(g) * u).astype(BF16)
    h = h + jnp.dot(act, wd_ref[0], preferred_element_type=F32)
    o_ref[...] = _rms(h, fw_ref[...]) if final_norm else h


def _proj_ffn(a, wp, res, norm_w, wgu, wd, final_w, *, layer, final_norm):
    T, D = res.shape
    rows = FFN_ROWS
    once = pl.Buffered(1)
    return pl.pallas_call(
        functools.partial(_proj_ffn_kernel, final_norm=final_norm),
        grid=(T // rows,),
        in_specs=[
            pl.BlockSpec((rows, D), lambda i: (i, 0)),
            pl.BlockSpec((1, D, D), lambda i: (0, 0, 0), pipeline_mode=once),
            pl.BlockSpec((rows, D), lambda i: (i, 0)),
            pl.BlockSpec((1, D), lambda i: (0, 0)),
            pl.BlockSpec((1, D, D_FF), lambda i: (layer, 0, 0), pipeline_mode=once),
            pl.BlockSpec((1, D, D_FF), lambda i: (layer, 0, 1), pipeline_mode=once),
            pl.BlockSpec((1, D_FF, D), lambda i: (layer, 0, 0), pipeline_mode=once),
            pl.BlockSpec((1, D), lambda i: (0, 0)),
        ],
        out_specs=pl.BlockSpec((rows, D), lambda i: (i, 0)),
        out_shape=jax.ShapeDtypeStruct((T, D), F32),
        compiler_params=_params(1),
        name="proj_ffn_final" if final_norm else "proj_ffn",
    )(a, wp, res, norm_w, wgu, wgu, wd, final_w)


def _mlstm_proj_kernel(x_ref, nw_ref, wqt_ref, wk_ref, wvt_ref, wot_ref, wgt_ref, bg_ref,
                       qt_ref, k_ref, vt_ref, ogt_ref, gt_ref):
    xn = _rms(x_ref[0], nw_ref[...]).astype(BF16)
    scale = MLSTM_QK_DIM ** -0.5
    nt = lambda w_ref: lax.dot_general(w_ref[...], xn, _NT, preferred_element_type=F32)
    qt_ref[0] = (nt(wqt_ref) * scale).astype(BF16)
    k_ref[0] = jnp.dot(xn, wk_ref[...], preferred_element_type=F32).astype(BF16)
    vt_ref[0] = nt(wvt_ref).astype(BF16)
    ogt_ref[0] = jax.nn.sigmoid(nt(wot_ref)).astype(BF16)
    gates = nt(wgt_ref) + bg_ref[...]
    gt_ref[0] = GATE_SOFTCAP * jnp.tanh(gates / GATE_SOFTCAP)


def _mlstm_proj(x, norm_w, wqt, wk, wvt, wot, wgt, bg):
    B, S, D = x.shape
    rows = PROJ_ROWS
    H, dk = MLSTM_HEADS, MLSTM_QK_DIM
    const = lambda b, i: (0, 0)
    tok = lambda w: pl.BlockSpec((1, rows, w), lambda b, i: (b, i, 0))
    tok_t = lambda w: pl.BlockSpec((1, w, rows), lambda b, i: (b, 0, i))
    return pl.pallas_call(
        _mlstm_proj_kernel,
        grid=(B, S // rows),
        in_specs=[
            tok(D),
            pl.BlockSpec((1, D), const),
            pl.BlockSpec((H * dk, D), const),
            pl.BlockSpec((D, H * dk), const),
            pl.BlockSpec((D, D), const),
            pl.BlockSpec((D, D), const),
            pl.BlockSpec((2 * H, D), const),
            pl.BlockSpec((2 * H, 1), const),
        ],
        out_specs=[tok_t(H * dk), tok(H * dk), tok_t(D), tok_t(D), tok_t(2 * H)],
        out_shape=[
            jax.ShapeDtypeStruct((B, H * dk, S), BF16),
            jax.ShapeDtypeStruct((B, S, H * dk), BF16),
            jax.ShapeDtypeStruct((B, D, S), BF16),
            jax.ShapeDtypeStruct((B, D, S), BF16),
            jax.ShapeDtypeStruct((B, 2 * H, S), F32),
        ],
        compiler_params=_params(2),
        name="mlstm_proj",
    )(x, norm_w, wqt, wk, wvt, wot, wgt, bg)


def _lane_scan(x, op, fill):
    lane = lax.broadcasted_iota(jnp.int32, (x.shape[0], LANES), 1)
    groups, carry = [], None
    for t in range(x.shape[1] // LANES):
        y = x[:, t * LANES:(t + 1) * LANES]
        sh = 1
        while sh < LANES:
            y = op(y, jnp.where(lane >= sh, pltpu.roll(y, sh, 1), fill))
            sh *= 2
        if carry is not None:
            y = op(y, carry)
        carry = y[:, LANES - 1:LANES]
        groups.append(y)
    return jnp.concatenate(groups, axis=1)


def _mlstm_core_kernel(qt_ref, k_ref, vt_ref, ogt_ref, gt_ref, hw_ref, y_ref, c_sc, m_sc):
    L = MLSTM_CHUNK
    H, dk, dv = MLSTM_HEADS, MLSTM_QK_DIM, MLSTM_V_DIM

    @pl.when(pl.program_id(1) == 0)
    def _():
        c_sc[...] = jnp.zeros_like(c_sc)
        m_sc[...] = jnp.zeros_like(m_sc)

    s_idx = lax.broadcasted_iota(jnp.int32, (L, L), 0)
    l_idx = lax.broadcasted_iota(jnp.int32, (L, L), 1)
    causal = s_idx <= l_idx
    feat = lax.broadcasted_iota(jnp.int32, (LANES, L), 0)
    one_rows = jnp.where(lax.broadcasted_iota(jnp.int32, (16, L), 0) == 0, 1.0, 0.0).astype(BF16)

    for c in range(qt_ref.shape[2] // L):
        cols = slice(c * L, (c + 1) * L)
        gates = gt_ref[0, :, cols]
        ipre, fpre = gates[0:H], gates[H:2 * H]
        logf = jnp.minimum(fpre, 0.0) - jnp.log1p(jnp.exp(-jnp.abs(fpre)))
        b = _lane_scan(logf, jnp.add, 0.0)
        u = ipre - b
        m_prev = m_sc[:, 0:1]
        big_m = jnp.maximum(_lane_scan(u, jnp.maximum, NEG_INF), m_prev)
        inter = jnp.exp(m_prev - big_m)
        floor = jnp.exp(-(b + big_m))
        m_last = big_m[:, L - 1:L]
        w = jnp.exp(u - m_last)
        decay = jnp.exp(m_prev - m_last)
        m_sc[...] = jnp.broadcast_to(b[:, L - 1:L] + m_last, (H, LANES))
        u_cols = jnp.concatenate([u * LOG2E, jnp.zeros((LANES - H, L), F32)], axis=0).T
        big_m2 = big_m * LOG2E

        for hp in range(H // 2):
            k2 = k_ref[0, cols, hp * LANES:(hp + 1) * LANES]
            qt2 = qt_ref[0, hp * LANES:(hp + 1) * LANES, cols]
            for a in range(2):
                h = 2 * hp + a
                qta = jnp.where((feat >= a * dk) & (feat < (a + 1) * dk), qt2, jnp.zeros_like(qt2))
                s_t = jnp.dot(k2, qta, preferred_element_type=F32)
                e_t = jnp.exp2(jnp.where(causal, u_cols[:, h:h + 1] - big_m2[h:h + 1], NEG_INF))
                v_ext = jnp.concatenate([vt_ref[0, h * dv:(h + 1) * dv, cols], one_rows], axis=0)
                state = c_sc[h]
                r = (inter[h:h + 1] * jnp.dot(state.astype(BF16), qta, preferred_element_type=F32)
                     + jnp.dot(v_ext, (s_t * e_t).astype(BF16), preferred_element_type=F32))
                hh = r[:dv] / jnp.maximum(jnp.abs(r[dv:dv + 1]), floor[h:h + 1])
                hn = hh * lax.rsqrt(jnp.mean(hh * hh, axis=0, keepdims=True) + RMS_EPS)
                hw = hw_ref[h * dv:(h + 1) * dv, :]
                hn = hn * jnp.concatenate([hw] * (L // LANES), axis=1)
                y_t = ogt_ref[0, h * dv:(h + 1) * dv, cols].astype(F32) * hn
                y_ref[0, cols, h * dv:(h + 1) * dv] = y_t.T.astype(BF16)

                wv = (v_ext.astype(F32) * w[h:h + 1]).astype(BF16)
                c_sc[h] = decay[h:h + 1] * state + jnp.dot(wv, k2, preferred_element_type=F32)


def _mlstm_core(qt, k, vt, ogt, gt, head_w):
    B, D, S = vt.shape
    rows = MLSTM_ROWS
    H, dk, dv = MLSTM_HEADS, MLSTM_QK_DIM, MLSTM_V_DIM
    tok_t = lambda w: pl.BlockSpec((1, w, rows), lambda b, i: (b, 0, i))
    return pl.pallas_call(
        _mlstm_core_kernel,
        grid=(B, S // rows),
        in_specs=[
            tok_t(H * dk),
            pl.BlockSpec((1, rows, H * dk), lambda b, i: (b, i, 0)),
            tok_t(D), tok_t(D), tok_t(2 * H),
            pl.BlockSpec((D, LANES), lambda b, i: (0, 0)),
        ],
        out_specs=pl.BlockSpec((1, rows, D), lambda b, i: (b, i, 0)),
        out_shape=jax.ShapeDtypeStruct((B, S, D), BF16),
        scratch_shapes=[pltpu.VMEM((H, dv + 16, LANES), F32), pltpu.VMEM((H, LANES), F32)],
        compiler_params=_params(2),
        name="mlstm_core",
    )(qt, k, vt, ogt, gt, head_w)


def _rotary_tables(seq):
    pos = jnp.arange(seq, dtype=F32)
    inv_freq = ROPE_THETA ** (-jnp.arange(0, ROT_DIM, 2, dtype=F32) / ROT_DIM)
    ang = pos[:, None] * inv_freq[None, :]
    cos, sin = jnp.cos(ang), jnp.sin(ang)
    d = jnp.arange(LANES) % ATTN_HEAD_DIM
    cos_l, sin_l = cos[:, d % ROT_HALF], sin[:, d % ROT_HALF]
    cn = jnp.where(d < ROT_DIM, cos_l, 1.0)
    sa = jnp.where(d < ROT_HALF, -sin_l, 0.0)
    sb = jnp.where((d >= ROT_HALF) & (d < ROT_DIM), sin_l, 0.0)
    return cn, sa, sb, cos.T, sin.T


def kernel(x, attn_norm, attn_w_qkv, attn_w_o, mlstm_norm, mlstm_w_in, mlstm_b_gates, mlstm_head_norm,
           mlstm_w_out, ffn_norm, ffn_w_gate_up, ffn_w_down, final_norm):
    B, S, D = x.shape
    T = B * S
    H, dk = MLSTM_HEADS, MLSTM_QK_DIM
    row = lambda w: w.reshape(1, -1).astype(F32)

    cn, sa, sb, cost, sint = _rotary_tables(S)
    wqkv = attn_w_qkv[0].astype(BF16)
    qt, k, vt, kmean = _attn_proj(x, row(attn_norm[0]), wqkv[:, :D].T, wqkv[:, D:2 * D], wqkv[:, 2 * D:].T,
                                  cn, sa, sb, cost, sint)
    o = _moba(qt, k, vt, kmean.reshape(B, S // MOBA_BLOCK, D))
    wgu, wd = ffn_w_gate_up.astype(BF16), ffn_w_down.astype(BF16)
    h = _proj_ffn(o.reshape(T, D), attn_w_o.astype(BF16), x.reshape(T, D), row(ffn_norm[0]), wgu, wd,
                  row(final_norm), layer=0, final_norm=False)

    w_in = mlstm_w_in[0].astype(BF16)
    q_end, k_end, v_end, o_end = H * dk, 2 * H * dk, 2 * H * dk + D, 2 * H * dk + 2 * D
    qt, k, vt, ogt, gt = _mlstm_proj(
        h.reshape(B, S, D), row(mlstm_norm[0]), w_in[:, :q_end].T, w_in[:, q_end:k_end],
        w_in[:, k_end:v_end].T, w_in[:, v_end:o_end].T, w_in[:, o_end:].T,
        mlstm_b_gates[0].reshape(2 * H, 1).astype(F32))
    head_w = jnp.broadcast_to(mlstm_head_norm[0].astype(F32).reshape(D, 1), (D, LANES))
    y = _mlstm_core(qt, k, vt, ogt, gt, head_w)
    out = _proj_ffn(y.reshape(T, D), mlstm_w_out.astype(BF16), h, row(ffn_norm[1]), wgu, wd,
                    row(final_norm), layer=1, final_norm=True)
    return out.reshape(B, S, D)
```

```python
import functools

import jax
import jax.numpy as jnp
from jax import lax
from jax.experimental import pallas as pl
from jax.experimental.pallas import tpu as pltpu

F32 = jnp.float32
BF16 = jnp.bfloat16

D_MODEL = 1024
ATTN_HEADS = 16
ATTN_HEAD_DIM = 64
ROT_DIM = 16
ROT_HALF = ROT_DIM // 2
ROPE_THETA = 500000.0
MOBA_BLOCK = 256
MOBA_TOPK = 3
MLSTM_HEADS = 8
MLSTM_QK_DIM = 64
MLSTM_V_DIM = 128
GATE_SOFTCAP = 15.0
D_FF = 2816
RMS_EPS = 1e-6

LANES = 128
VMEM_LIMIT_BYTES = 56 * 1024 * 1024
PROJ_ROWS = 512
FFN_ROWS = 512
MLSTM_CHUNK = 256
MLSTM_ROWS = 1024
MOBA_HEADS = 4
MOBA_STEP = 8
LOG2E = 1.4426950408889634
NEG_INF = float("-inf")

_NT = (((1,), (1,)), ((), ()))


def _params(n_grid):
    return pltpu.CompilerParams(dimension_semantics=("arbitrary",) * n_grid,
                                vmem_limit_bytes=VMEM_LIMIT_BYTES)


def _rms(x, w):
    return x * lax.rsqrt(jnp.mean(x * x, axis=-1, keepdims=True) + RMS_EPS) * w


def _attn_proj_kernel(x_ref, nw_ref, wqt_ref, wk_ref, wvt_ref, cn_ref, sa_ref, sb_ref, cost_ref,
                      sint_ref, qt_ref, k_ref, vt_ref, kmean_ref):
    rows = x_ref.shape[1]
    xn = _rms(x_ref[0], nw_ref[...]).astype(BF16)

    k = jnp.dot(xn, wk_ref[...], preferred_element_type=F32)
    cn, sa, sb = cn_ref[...], sa_ref[...], sb_ref[...]
    for g in range(D_MODEL // LANES):
        kg = k[:, g * LANES:(g + 1) * LANES]
        kr = (kg * cn + pltpu.roll(kg, LANES - ROT_HALF, 1) * sa + pltpu.roll(kg, ROT_HALF, 1) * sb)
        k_ref[0, :, g * LANES:(g + 1) * LANES] = kr.astype(BF16)
        for t in range(rows // MOBA_BLOCK):
            kmean_ref[0, 0, t:t + 1, g * LANES:(g + 1) * LANES] = jnp.mean(
                kr[t * MOBA_BLOCK:(t + 1) * MOBA_BLOCK], axis=0, keepdims=True)

    qt = lax.dot_general(wqt_ref[...], xn, _NT, preferred_element_type=F32)
    ct, st = cost_ref[...], sint_ref[...]
    scale = ATTN_HEAD_DIM ** -0.5 * LOG2E
    for h in range(ATTN_HEADS):
        blk = qt[h * ATTN_HEAD_DIM:(h + 1) * ATTN_HEAD_DIM]
        x1, x2 = blk[0:ROT_HALF], blk[ROT_HALF:ROT_DIM]
        r = jnp.concatenate([x1 * ct - x2 * st, x2 * ct + x1 * st, blk[ROT_DIM:]], axis=0) * scale
        qt_ref[0, h * ATTN_HEAD_DIM:(h + 1) * ATTN_HEAD_DIM, :] = r.astype(BF16)

    vt = lax.dot_general(wvt_ref[...], xn, _NT, preferred_element_type=F32)
    for t in range(rows // MOBA_BLOCK):
        vt_ref[0, t] = vt[:, t * MOBA_BLOCK:(t + 1) * MOBA_BLOCK].astype(BF16)


def _attn_proj(x, norm_w, wqt, wk, wvt, cn, sa, sb, cost, sint):
    B, S, D = x.shape
    rows = PROJ_ROWS
    nb_step = rows // MOBA_BLOCK
    const = lambda b, i: (0, 0)
    return pl.pallas_call(
        _attn_proj_kernel,
        grid=(B, S // rows),
        in_specs=[
            pl.BlockSpec((1, rows, D), lambda b, i: (b, i, 0)),
            pl.BlockSpec((1, D), const),
            pl.BlockSpec((D, D), const),
            pl.BlockSpec((D, D), const),
            pl.BlockSpec((D, D), const),
            pl.BlockSpec((rows, LANES), lambda b, i: (i, 0)),
            pl.BlockSpec((rows, LANES), lambda b, i: (i, 0)),
            pl.BlockSpec((rows, LANES), lambda b, i: (i, 0)),
            pl.BlockSpec((ROT_HALF, rows), lambda b, i: (0, i)),
            pl.BlockSpec((ROT_HALF, rows), lambda b, i: (0, i)),
        ],
        out_specs=[
            pl.BlockSpec((1, D, rows), lambda b, i: (b, 0, i)),
            pl.BlockSpec((1, rows, D), lambda b, i: (b, i, 0)),
            pl.BlockSpec((1, nb_step, D, MOBA_BLOCK), lambda b, i: (b, i, 0, 0)),
            pl.BlockSpec((1, 1, nb_step, D), lambda b, i: (b, i, 0, 0)),
        ],
        out_shape=[
            jax.ShapeDtypeStruct((B, D, S), BF16),
            jax.ShapeDtypeStruct((B, S, D), BF16),
            jax.ShapeDtypeStruct((B, S // MOBA_BLOCK, D, MOBA_BLOCK), BF16),
            jax.ShapeDtypeStruct((B, S // rows, nb_step, D), F32),
        ],
        compiler_params=_params(2),
        name="attn_proj",
    )(x, norm_w, wqt, wk, wvt, cn, sa, sb, cost, sint)


def _moba_kernel(qt_ref, k_ref, vt_ref, km_ref, o_ref, sel_ref, s_sc):
    cur = pl.program_id(2)
    nblk = km_ref.shape[1]
    blk_q = qt_ref.shape[2]
    hd = ATTN_HEAD_DIM
    heads = range(MOBA_HEADS)
    row = lax.broadcasted_iota(jnp.int32, (LANES, blk_q), 0)
    blk = lax.broadcasted_iota(jnp.int32, (nblk, blk_q), 0)
    kpos = lax.broadcasted_iota(jnp.int32, (MOBA_BLOCK, blk_q), 0)
    qpos = lax.broadcasted_iota(jnp.int32, (MOBA_BLOCK, blk_q), 1)
    ones_rows = jnp.ones((16, MOBA_BLOCK), BF16)

    qpads = []
    for a in heads:
        g, r = a // 2, a % 2
        qt2 = qt_ref[0, g * LANES:(g + 1) * LANES, :]
        qta = jnp.where((row >= r * hd) & (row < (r + 1) * hd), qt2, jnp.zeros_like(qt2))
        qpads.append(qta)
        km = km_ref[0, :, g * LANES:(g + 1) * LANES].astype(BF16)
        gate = jnp.dot(km, qta, preferred_element_type=F32)
        gate = jnp.where(blk < cur, gate, NEG_INF)
        sel = jnp.zeros((nblk, blk_q), F32)
        for _ in range(MOBA_TOPK):
            mx = jnp.max(gate, axis=0, keepdims=True)
            idx = jnp.min(jnp.where(gate == mx, blk, nblk), axis=0, keepdims=True)
            hit = blk == idx
            sel = jnp.where(hit & (blk < cur), 1.0, sel)
            gate = jnp.where(hit, NEG_INF, gate)
        sel_ref[a] = sel

    def scores(j):
        jj = jnp.minimum(j, nblk - 1)
        rows = pl.ds(pl.multiple_of(jj * MOBA_BLOCK, MOBA_BLOCK), MOBA_BLOCK)
        return [jnp.dot(k_ref[0, rows, (a // 2) * LANES:(a // 2 + 1) * LANES], qpads[a],
                        preferred_element_type=F32) for a in heads]

    def v_ext(j, a):
        return jnp.concatenate([vt_ref[0, j, a * hd:(a + 1) * hd, :], ones_rows], axis=0)

    m_acc = []
    for a, s in enumerate(scores(cur)):
        s = jnp.where(kpos <= qpos, s, NEG_INF)
        m = jnp.max(s, axis=0, keepdims=True)
        p = jnp.exp2(s - m).astype(BF16)
        m_acc += [m, jnp.dot(v_ext(cur, a), p, preferred_element_type=F32)]

    def park(s, slot):
        for a in heads:
            s_sc[slot, a] = s[a]
        return [jnp.max(s[a], axis=0, keepdims=True) for a in heads]

    def consume(j, slot, smax, m_acc):
        jj = jnp.minimum(j, nblk - 1)
        out = []
        for a in heads:
            m, acc = m_acc[2 * a], m_acc[2 * a + 1]
            chosen = sel_ref[a, pl.ds(jj, 1), :] > 0.5
            m_new = jnp.where(chosen, jnp.maximum(m, smax[a]), m)
            p = jnp.exp2(s_sc[slot, a] - jnp.where(chosen, m_new, jnp.inf)).astype(BF16)
            acc = jnp.exp2(m - m_new) * acc + jnp.dot(v_ext(jj, a), p, preferred_element_type=F32)
            out += [m_new, acc]
        return out

    nh = MOBA_HEADS

    def steps(n, base):
        def body(i, carry):
            m_acc = list(carry[:2 * nh])
            smax = {0: list(carry[2 * nh:3 * nh]), 1: list(carry[3 * nh:])}
            j = base + n * i
            for t in range(n):
                s_next = scores(j + t + 2)
                m_acc = consume(j + t, t, smax[t], m_acc)
                smax[(t + 2) % n] = park(s_next, (t + 2) % n)
            return tuple(m_acc + smax[0] + smax[1])
        return body

    init = tuple(m_acc + park(scores(0), 0) + park(scores(1), 1))
    full = cur // MOBA_STEP
    carry = lax.fori_loop(0, full, steps(MOBA_STEP, 0), init)
    left = cur - MOBA_STEP * full
    carry = lax.fori_loop(0, (left + 1) // 2, steps(2, MOBA_STEP * full), carry)
    outs = [carry[2 * a + 1][:hd] / carry[2 * a + 1][hd:hd + 1] for a in heads]
    o_ref[0] = jnp.concatenate(outs, axis=0).T.astype(BF16)


def _moba(qt, k, vt, kmean):
    B, D, S = qt.shape
    nblk = S // MOBA_BLOCK
    width = MOBA_HEADS * ATTN_HEAD_DIM
    return pl.pallas_call(
        _moba_kernel,
        grid=(B, D // width, nblk),
        in_specs=[
            pl.BlockSpec((1, width, MOBA_BLOCK), lambda b, p, i: (b, p, i)),
            pl.BlockSpec((1, S, width), lambda b, p, i: (b, 0, p)),
            pl.BlockSpec((1, nblk, width, MOBA_BLOCK), lambda b, p, i: (b, 0, p, 0)),
            pl.BlockSpec((1, nblk, width), lambda b, p, i: (b, 0, p)),
        ],
        out_specs=pl.BlockSpec((1, MOBA_BLOCK, width), lambda b, p, i: (b, i, p)),
        out_shape=jax.ShapeDtypeStruct((B, S, D), BF16),
        scratch_shapes=[pltpu.VMEM((MOBA_HEADS, nblk, MOBA_BLOCK), F32),
                        pltpu.VMEM((MOBA_STEP, MOBA_HEADS, MOBA_BLOCK, MOBA_BLOCK), F32)],
        compiler_params=_params(3),
        name="moba",
    )(qt, k, vt, kmean)


def _proj_ffn_kernel(a_ref, wp_ref, res_ref, nw_ref, wg_ref, wu_ref, wd_ref, fw_ref, o_ref, *, final_norm):
    h = jnp.dot(a_ref[...], wp_ref[0], preferred_element_type=F32) + res_ref[...]
    xn = _rms(h, nw_ref[...]).astype(BF16)
    g = jnp.dot(xn, wg_ref[0], preferred_element_type=F32)
    u = jnp.dot(xn, wu_ref[0], preferred_element_type=F32)
    act = (g * jax.nn.sigmoid(g) * u).astype(BF16)
    h = h + jnp.dot(act, wd_ref[0], preferred_element_type=F32)
    o_ref[...] = _rms(h, fw_ref[...]) if final_norm else h


def _proj_ffn(a, wp, res, norm_w, wgu, wd, final_w, *, layer, final_norm):
    T, D = res.shape
    rows = FFN_ROWS
    once = pl.Buffered(1)
    return pl.pallas_call(
        functools.partial(_proj_ffn_kernel, final_norm=final_norm),
        grid=(T // rows,),
        in_specs=[
            pl.BlockSpec((rows, D), lambda i: (i, 0)),
            pl.BlockSpec((1, D, D), lambda i: (0, 0, 0), pipeline_mode=once),
            pl.BlockSpec((rows, D), lambda i: (i, 0)),
            pl.BlockSpec((1, D), lambda i: (0, 0)),
            pl.BlockSpec((1, D, D_FF), lambda i: (layer, 0, 0), pipeline_mode=once),
            pl.BlockSpec((1, D, D_FF), lambda i: (layer, 0, 1), pipeline_mode=once),
            pl.BlockSpec((1, D_FF, D), lambda i: (layer, 0, 0), pipeline_mode=once),
            pl.BlockSpec((1, D), lambda i: (0, 0)),
        ],
        out_specs=pl.BlockSpec((rows, D), lambda i: (i, 0)),
        out_shape=jax.ShapeDtypeStruct((T, D), F32),
        compiler_params=_params(1),
        name="proj_ffn_final" if final_norm else "proj_ffn",
    )(a, wp, res, norm_w, wgu, wgu, wd, final_w)


def _mlstm_proj_kernel(x_ref, nw_ref, wqt_ref, wk_ref, wvt_ref, wot_ref, wgt_ref, bg_ref,
                       qt_ref, k_ref, vt_ref, ogt_ref, gt_ref):
    xn = _rms(x_ref[0], nw_ref[...]).astype(BF16)
    scale = MLSTM_QK_DIM ** -0.5
    nt = lambda w_ref: lax.dot_general(w_ref[...], xn, _NT, preferred_element_type=F32)
    qt_ref[0] = (nt(wqt_ref) * scale).astype(BF16)
    k_ref[0] = jnp.dot(xn, wk_ref[...], preferred_element_type=F32).astype(BF16)
    vt_ref[0] = nt(wvt_ref).astype(BF16)
    ogt_ref[0] = jax.nn.sigmoid(nt(wot_ref)).astype(BF16)
    gates = nt(wgt_ref) + bg_ref[...]
    gt_ref[0] = GATE_SOFTCAP * jnp.tanh(gates / GATE_SOFTCAP)


def _mlstm_proj(x, norm_w, wqt, wk, wvt, wot, wgt, bg):
    B, S, D = x.shape
    rows = PROJ_ROWS
    H, dk = MLSTM_HEADS, MLSTM_QK_DIM
    const = lambda b, i: (0, 0)
    tok = lambda w: pl.BlockSpec((1, rows, w), lambda b, i: (b, i, 0))
    tok_t = lambda w: pl.BlockSpec((1, w, rows), lambda b, i: (b, 0, i))
    return pl.pallas_call(
        _mlstm_proj_kernel,
        grid=(B, S // rows),
        in_specs=[
            tok(D),
            pl.BlockSpec((1, D), const),
            pl.BlockSpec((H * dk, D), const),
            pl.BlockSpec((D, H * dk), const),
            pl.BlockSpec((D, D), const),
            pl.BlockSpec((D, D), const),
            pl.BlockSpec((2 * H, D), const),
            pl.BlockSpec((2 * H, 1), const),
        ],
        out_specs=[tok_t(H * dk), tok(H * dk), tok_t(D), tok_t(D), tok_t(2 * H)],
        out_shape=[
            jax.ShapeDtypeStruct((B, H * dk, S), BF16),
            jax.ShapeDtypeStruct((B, S, H * dk), BF16),
            jax.ShapeDtypeStruct((B, D, S), BF16),
            jax.ShapeDtypeStruct((B, D, S), BF16),
            jax.ShapeDtypeStruct((B, 2 * H, S), F32),
        ],
        compiler_params=_params(2),
        name="mlstm_proj",
    )(x, norm_w, wqt, wk, wvt, wot, wgt, bg)


def _lane_scan(x, op, fill):
    lane = lax.broadcasted_iota(jnp.int32, (x.shape[0], LANES), 1)
    groups, carry = [], None
    for t in range(x.shape[1] // LANES):
        y = x[:, t * LANES:(t + 1) * LANES]
        sh = 1
        while sh < LANES:
            y = op(y, jnp.where(lane >= sh, pltpu.roll(y, sh, 1), fill))
            sh *= 2
        if carry is not None:
            y = op(y, carry)
        carry = y[:, LANES - 1:LANES]
        groups.append(y)
    return jnp.concatenate(groups, axis=1)


def _mlstm_core_kernel(qt_ref, k_ref, vt_ref, ogt_ref, gt_ref, hw_ref, y_ref, c_sc, m_sc):
    L = MLSTM_CHUNK
    H, dk, dv = MLSTM_HEADS, MLSTM_QK_DIM, MLSTM_V_DIM

    @pl.when(pl.program_id(1) == 0)
    def _():
        c_sc[...] = jnp.zeros_like(c_sc)
        m_sc[...] = jnp.zeros_like(m_sc)

    s_idx = lax.broadcasted_iota(jnp.int32, (L, L), 0)
    l_idx = lax.broadcasted_iota(jnp.int32, (L, L), 1)
    causal = s_idx <= l_idx
    feat = lax.broadcasted_iota(jnp.int32, (LANES, L), 0)
    one_rows = jnp.where(lax.broadcasted_iota(jnp.int32, (16, L), 0) == 0, 1.0, 0.0).astype(BF16)

    for c in range(qt_ref.shape[2] // L):
        cols = slice(c * L, (c + 1) * L)
        gates = gt_ref[0, :, cols]
        ipre, fpre = gates[0:H], gates[H:2 * H]
        logf = jnp.minimum(fpre, 0.0) - jnp.log1p(jnp.exp(-jnp.abs(fpre)))
        b = _lane_scan(logf, jnp.add, 0.0)
        u = ipre - b
        m_prev = m_sc[:, 0:1]
        big_m = jnp.maximum(_lane_scan(u, jnp.maximum, NEG_INF), m_prev)
        inter = jnp.exp(m_prev - big_m)
        floor = jnp.exp(-(b + big_m))
        m_last = big_m[:, L - 1:L]
        w = jnp.exp(u - m_last)
        decay = jnp.exp(m_prev - m_last)
        m_sc[...] = jnp.broadcast_to(b[:, L - 1:L] + m_last, (H, LANES))
        u_cols = jnp.concatenate([u * LOG2E, jnp.zeros((LANES - H, L), F32)], axis=0).T
        big_m2 = big_m * LOG2E

        for hp in range(H // 2):
            k2 = k_ref[0, cols, hp * LANES:(hp + 1) * LANES]
            qt2 = qt_ref[0, hp * LANES:(hp + 1) * LANES, cols]
            for a in range(2):
                h = 2 * hp + a
                qta = jnp.where((feat >= a * dk) & (feat < (a + 1) * dk), qt2, jnp.zeros_like(qt2))
                s_t = jnp.dot(k2, qta, preferred_element_type=F32)
                e_t = jnp.exp2(jnp.where(causal, u_cols[:, h:h + 1] - big_m2[h:h + 1], NEG_INF))
                v_ext = jnp.concatenate([vt_ref[0, h * dv:(h + 1) * dv, cols], one_rows], axis=0)
                state = c_sc[h]
                r = (inter[h:h + 1] * jnp.dot(state.astype(BF16), qta, preferred_element_type=F32)
                     + jnp.dot(v_ext, (s_t * e_t).astype(BF16), preferred_element_type=F32))
                hh = r[:dv] / jnp.maximum(jnp.abs(r[dv:dv + 1]), floor[h:h + 1])
                hn = hh * lax.rsqrt(jnp.mean(hh * hh, axis=0, keepdims=True) + RMS_EPS)
                hw = hw_ref[h * dv:(h + 1) * dv, :]
                hn = hn * jnp.concatenate([hw] * (L // LANES), axis=1)
                y_t = ogt_ref[0, h * dv:(h + 1) * dv, cols].astype(F32) * hn
                y_ref[0, cols, h * dv:(h + 1) * dv] = y_t.T.astype(BF16)

                wv = (v_ext.astype(F32) * w[h:h + 1]).astype(BF16)
                c_sc[h] = decay[h:h + 1] * state + jnp.dot(wv, k2, preferred_element_type=F32)


def _mlstm_core(qt, k, vt, ogt, gt, head_w):
    B, D, S = vt.shape
    rows = MLSTM_ROWS
    H, dk, dv = MLSTM_HEADS, MLSTM_QK_DIM, MLSTM_V_DIM
    tok_t = lambda w: pl.BlockSpec((1, w, rows), lambda b, i: (b, 0, i))
    return pl.pallas_call(
        _mlstm_core_kernel,
        grid=(B, S // rows),
        in_specs=[
            tok_t(H * dk),
            pl.BlockSpec((1, rows, H * dk), lambda b, i: (b, i, 0)),
            tok_t(D), tok_t(D), tok_t(2 * H),
            pl.BlockSpec((D, LANES), lambda b, i: (0, 0)),
        ],
        out_specs=pl.BlockSpec((1, rows, D), lambda b, i: (b, i, 0)),
        out_shape=jax.ShapeDtypeStruct((B, S, D), BF16),
        scratch_shapes=[pltpu.VMEM((H, dv + 16, LANES), F32), pltpu.VMEM((H, LANES), F32)],
        compiler_params=_params(2),
        name="mlstm_core",
    )(qt, k, vt, ogt, gt, head_w)


def _rotary_tables(seq):
    pos = jnp.arange(seq, dtype=F32)
    inv_freq = ROPE_THETA ** (-jnp.arange(0, ROT_DIM, 2, dtype=F32) / ROT_DIM)
    ang = pos[:, None] * inv_freq[None, :]
    cos, sin = jnp.cos(ang), jnp.sin(ang)
    d = jnp.arange(LANES) % ATTN_HEAD_DIM
    cos_l, sin_l = cos[:, d % ROT_HALF], sin[:, d % ROT_HALF]
    cn = jnp.where(d < ROT_DIM, cos_l, 1.0)
    sa = jnp.where(d < ROT_HALF, -sin_l, 0.0)
    sb = jnp.where((d >= ROT_HALF) & (d < ROT_DIM), sin_l, 0.0)
    return cn, sa, sb, cos.T, sin.T


def kernel(x, attn_norm, attn_w_qkv, attn_w_o, mlstm_norm, mlstm_w_in, mlstm_b_gates, mlstm_head_norm,
           mlstm_w_out, ffn_norm, ffn_w_gate_up, ffn_w_down, final_norm):
    B, S, D = x.shape
    T = B * S
    H, dk = MLSTM_HEADS, MLSTM_QK_DIM
    row = lambda w: w.reshape(1, -1).astype(F32)

    cn, sa, sb, cost, sint = _rotary_tables(S)
    wqkv = attn_w_qkv[0].astype(BF16)
    qt, k, vt, kmean = _attn_proj(x, row(attn_norm[0]), wqkv[:, :D].T, wqkv[:, D:2 * D], wqkv[:, 2 * D:].T,
                                  cn, sa, sb, cost, sint)
    o = _moba(qt, k, vt, kmean.reshape(B, S // MOBA_BLOCK, D))
    wgu, wd = ffn_w_gate_up.astype(BF16), ffn_w_down.astype(BF16)
    h = _proj_ffn(o.reshape(T, D), attn_w_o.astype(BF16), x.reshape(T, D), row(ffn_norm[0]), wgu, wd,
                  row(final_norm), layer=0, final_norm=False)

    w_in = mlstm_w_in[0].astype(BF16)
    q_end, k_end, v_end, o_end = H * dk, 2 * H * dk, 2 * H * dk + D, 2 * H * dk + 2 * D
    qt, k, vt, ogt, gt = _mlstm_proj(
        h.reshape(B, S, D), row(mlstm_norm[0]), w_in[:, :q_end].T, w_in[:, q_end:k_end],
        w_in[:, k_end:v_end].T, w_in[:, v_end:o_end].T, w_in[:, o_end:].T,
        mlstm_b_gates[0].reshape(2 * H, 1).astype(F32))
    head_w = jnp.broadcast_to(mlstm_head_norm[0].astype(F32).reshape(D, 1), (D, LANES))
    y = _mlstm_core(qt, k, vt, ogt, gt, head_w)
    out = _proj_ffn(y.reshape(T, D), mlstm_w_out.astype(BF16), h, row(ffn_norm[1]), wgu, wd,
                    row(final_norm), layer=1, final_norm=True)
    return out.reshape(B, S, D)
```

```python
import functools

import jax
import jax.numpy as jnp
from jax import lax
from jax.experimental import pallas as pl
from jax.experimental.pallas import tpu as pltpu

F32 = jnp.float32
BF16 = jnp.bfloat16

D_MODEL = 1024
ATTN_HEADS = 16
ATTN_HEAD_DIM = 64
ROT_DIM = 16
ROT_HALF = ROT_DIM // 2
ROPE_THETA = 500000.0
MOBA_BLOCK = 256
MOBA_TOPK = 3
MLSTM_HEADS = 8
MLSTM_QK_DIM = 64
MLSTM_V_DIM = 128
GATE_SOFTCAP = 15.0
D_FF = 2816
RMS_EPS = 1e-6

LANES = 128
VMEM_LIMIT_BYTES = 56 * 1024 * 1024
PROJ_ROWS = 512
FFN_ROWS = 512
MLSTM_CHUNK = 256
MLSTM_ROWS = 2048
MOBA_HEADS = 4
MOBA_STEP = 8
LOG2E = 1.4426950408889634
NEG_INF = float("-inf")

_NT = (((1,), (1,)), ((), ()))


def _params(n_grid):
    return pltpu.CompilerParams(dimension_semantics=("arbitrary",) * n_grid,
                                vmem_limit_bytes=VMEM_LIMIT_BYTES)


def _rms(x, w):
    return x * lax.rsqrt(jnp.mean(x * x, axis=-1, keepdims=True) + RMS_EPS) * w


def _attn_proj_kernel(x_ref, nw_ref, wqt_ref, wk_ref, wvt_ref, cn_ref, sa_ref, sb_ref, cost_ref,
                      sint_ref, qt_ref, k_ref, vt_ref, kmean_ref):
    rows = x_ref.shape[1]
    xn = _rms(x_ref[0], nw_ref[...]).astype(BF16)

    k = jnp.dot(xn, wk_ref[...], preferred_element_type=F32)
    cn, sa, sb = cn_ref[...], sa_ref[...], sb_ref[...]
    for g in range(D_MODEL // LANES):
        kg = k[:, g * LANES:(g + 1) * LANES]
        kr = (kg * cn + pltpu.roll(kg, LANES - ROT_HALF, 1) * sa + pltpu.roll(kg, ROT_HALF, 1) * sb)
        k_ref[0, :, g * LANES:(g + 1) * LANES] = kr.astype(BF16)
        for t in range(rows // MOBA_BLOCK):
            kmean_ref[0, 0, t:t + 1, g * LANES:(g + 1) * LANES] = jnp.mean(
                kr[t * MOBA_BLOCK:(t + 1) * MOBA_BLOCK], axis=0, keepdims=True)

    qt = lax.dot_general(wqt_ref[...], xn, _NT, preferred_element_type=F32)
    ct, st = cost_ref[...], sint_ref[...]
    scale = ATTN_HEAD_DIM ** -0.5 * LOG2E
    for h in range(ATTN_HEADS):
        blk = qt[h * ATTN_HEAD_DIM:(h + 1) * ATTN_HEAD_DIM]
        x1, x2 = blk[0:ROT_HALF], blk[ROT_HALF:ROT_DIM]
        r = jnp.concatenate([x1 * ct - x2 * st, x2 * ct + x1 * st, blk[ROT_DIM:]], axis=0) * scale
        qt_ref[0, h * ATTN_HEAD_DIM:(h + 1) * ATTN_HEAD_DIM, :] = r.astype(BF16)

    vt = lax.dot_general(wvt_ref[...], xn, _NT, preferred_element_type=F32)
    for t in range(rows // MOBA_BLOCK):
        vt_ref[0, t] = vt[:, t * MOBA_BLOCK:(t + 1) * MOBA_BLOCK].astype(BF16)


def _attn_proj(x, norm_w, wqt, wk, wvt, cn, sa, sb, cost, sint):
    B, S, D = x.shape
    rows = PROJ_ROWS
    nb_step = rows // MOBA_BLOCK
    const = lambda b, i: (0, 0)
    return pl.pallas_call(
        _attn_proj_kernel,
        grid=(B, S // rows),
        in_specs=[
            pl.BlockSpec((1, rows, D), lambda b, i: (b, i, 0)),
            pl.BlockSpec((1, D), const),
            pl.BlockSpec((D, D), const),
            pl.BlockSpec((D, D), const),
            pl.BlockSpec((D, D), const),
            pl.BlockSpec((rows, LANES), lambda b, i: (i, 0)),
            pl.BlockSpec((rows, LANES), lambda b, i: (i, 0)),
            pl.BlockSpec((rows, LANES), lambda b, i: (i, 0)),
            pl.BlockSpec((ROT_HALF, rows), lambda b, i: (0, i)),
            pl.BlockSpec((ROT_HALF, rows), lambda b, i: (0, i)),
        ],
        out_specs=[
            pl.BlockSpec((1, D, rows), lambda b, i: (b, 0, i)),
            pl.BlockSpec((1, rows, D), lambda b, i: (b, i, 0)),
            pl.BlockSpec((1, nb_step, D, MOBA_BLOCK), lambda b, i: (b, i, 0, 0)),
            pl.BlockSpec((1, 1, nb_step, D), lambda b, i: (b, i, 0, 0)),
        ],
        out_shape=[
            jax.ShapeDtypeStruct((B, D, S), BF16),
            jax.ShapeDtypeStruct((B, S, D), BF16),
            jax.ShapeDtypeStruct((B, S // MOBA_BLOCK, D, MOBA_BLOCK), BF16),
            jax.ShapeDtypeStruct((B, S // rows, nb_step, D), F32),
        ],
        compiler_params=_params(2),
        name="attn_proj",
    )(x, norm_w, wqt, wk, wvt, cn, sa, sb, cost, sint)


def _moba_kernel(qt_ref, k_ref, vt_ref, km_ref, o_ref, sel_ref, s_sc):
    cur = pl.program_id(2)
    nblk = km_ref.shape[1]
    blk_q = qt_ref.shape[2]
    hd = ATTN_HEAD_DIM
    heads = range(MOBA_HEADS)
    row = lax.broadcasted_iota(jnp.int32, (LANES, blk_q), 0)
    blk = lax.broadcasted_iota(jnp.int32, (nblk, blk_q), 0)
    kpos = lax.broadcasted_iota(jnp.int32, (MOBA_BLOCK, blk_q), 0)
    qpos = lax.broadcasted_iota(jnp.int32, (MOBA_BLOCK, blk_q), 1)
    ones_rows = jnp.ones((16, MOBA_BLOCK), BF16)

    qpads = []
    for a in heads:
        g, r = a // 2, a % 2
        qt2 = qt_ref[0, g * LANES:(g + 1) * LANES, :]
        qta = jnp.where((row >= r * hd) & (row < (r + 1) * hd), qt2, jnp.zeros_like(qt2))
        qpads.append(qta)
        km = km_ref[0, :, g * LANES:(g + 1) * LANES].astype(BF16)
        gate = jnp.dot(km, qta, preferred_element_type=F32)
        gate = jnp.where(blk < cur, gate, NEG_INF)
        sel = jnp.zeros((nblk, blk_q), F32)
        for _ in range(MOBA_TOPK):
            mx = jnp.max(gate, axis=0, keepdims=True)
            idx = jnp.min(jnp.where(gate == mx, blk, nblk), axis=0, keepdims=True)
            hit = blk == idx
            sel = jnp.where(hit & (blk < cur), 1.0, sel)
            gate = jnp.where(hit, NEG_INF, gate)
        sel_ref[a] = sel

    def scores(j):
        jj = jnp.minimum(j, nblk - 1)
        rows = pl.ds(pl.multiple_of(jj * MOBA_BLOCK, MOBA_BLOCK), MOBA_BLOCK)
        return [jnp.dot(k_ref[0, rows, (a // 2) * LANES:(a // 2 + 1) * LANES], qpads[a],
                        preferred_element_type=F32) for a in heads]

    def v_ext(j, a):
        return jnp.concatenate([vt_ref[0, j, a * hd:(a + 1) * hd, :], ones_rows], axis=0)

    m_acc = []
    for a, s in enumerate(scores(cur)):
        s = jnp.where(kpos <= qpos, s, NEG_INF)
        m = jnp.max(s, axis=0, keepdims=True)
        p = jnp.exp2(s - m).astype(BF16)
        m_acc += [m, jnp.dot(v_ext(cur, a), p, preferred_element_type=F32)]

    def park(s, slot):
        for a in heads:
            s_sc[slot, a] = s[a]
        return [jnp.max(s[a], axis=0, keepdims=True) for a in heads]

    def consume(j, slot, smax, m_acc):
        jj = jnp.minimum(j, nblk - 1)
        out = []
        for a in heads:
            m, acc = m_acc[2 * a], m_acc[2 * a + 1]
            chosen = sel_ref[a, pl.ds(jj, 1), :] > 0.5
            m_new = jnp.where(chosen, jnp.maximum(m, smax[a]), m)
            p = jnp.exp2(s_sc[slot, a] - jnp.where(chosen, m_new, jnp.inf)).astype(BF16)
            acc = jnp.exp2(m - m_new) * acc + jnp.dot(v_ext(jj, a), p, preferred_element_type=F32)
            out += [m_new, acc]
        return out

    nh = MOBA_HEADS

    def steps(n, base):
        def body(i, carry):
            m_acc = list(carry[:2 * nh])
            smax = {0: list(carry[2 * nh:3 * nh]), 1: list(carry[3 * nh:])}
            j = base + n * i
            for t in range(n):
                s_next = scores(j + t + 2)
                m_acc = consume(j + t, t, smax[t], m_acc)
                smax[(t + 2) % n] = park(s_next, (t + 2) % n)
            return tuple(m_acc + smax[0] + smax[1])
        return body

    init = tuple(m_acc + park(scores(0), 0) + park(scores(1), 1))
    full = cur // MOBA_STEP
    carry = lax.fori_loop(0, full, steps(MOBA_STEP, 0), init)
    left = cur - MOBA_STEP * full
    carry = lax.fori_loop(0, (left + 1) // 2, steps(2, MOBA_STEP * full), carry)
    outs = [carry[2 * a + 1][:hd] / carry[2 * a + 1][hd:hd + 1] for a in heads]
    o_ref[0] = jnp.concatenate(outs, axis=0).T.astype(BF16)


def _moba(qt, k, vt, kmean):
    B, D, S = qt.shape
    nblk = S // MOBA_BLOCK
    width = MOBA_HEADS * ATTN_HEAD_DIM
    return pl.pallas_call(
        _moba_kernel,
        grid=(B, D // width, nblk),
        in_specs=[
            pl.BlockSpec((1, width, MOBA_BLOCK), lambda b, p, i: (b, p, i)),
            pl.BlockSpec((1, S, width), lambda b, p, i: (b, 0, p)),
            pl.BlockSpec((1, nblk, width, MOBA_BLOCK), lambda b, p, i: (b, 0, p, 0)),
            pl.BlockSpec((1, nblk, width), lambda b, p, i: (b, 0, p)),
        ],
        out_specs=pl.BlockSpec((1, MOBA_BLOCK, width), lambda b, p, i: (b, i, p)),
        out_shape=jax.ShapeDtypeStruct((B, S, D), BF16),
        scratch_shapes=[pltpu.VMEM((MOBA_HEADS, nblk, MOBA_BLOCK), F32),
                        pltpu.VMEM((MOBA_STEP, MOBA_HEADS, MOBA_BLOCK, MOBA_BLOCK), F32)],
        compiler_params=_params(3),
        name="moba",
    )(qt, k, vt, kmean)


def _proj_ffn_kernel(a_ref, wp_ref, res_ref, nw_ref, wg_ref, wu_ref, wd_ref, fw_ref, o_ref, *, final_norm):
    h = jnp.dot(a_ref[...], wp_ref[0], preferred_element_type=F32) + res_ref[...]
    xn = _rms(h, nw_ref[...]).astype(BF16)
    g = jnp.dot(xn, wg_ref[0], preferred_element_type=F32)
    u = jnp.dot(xn, wu_ref[0], preferred_element_type=F32)
    act = (g * jax.nn.sigmoid(g) * u).astype(BF16)
    h = h + jnp.dot(act, wd_ref[0], preferred_element_type=F32)
    o_ref[...] = _rms(h, fw_ref[...]) if final_norm else h


def _proj_ffn(a, wp, res, norm_w, wgu, wd, final_w, *, layer, final_norm):
    T, D = res.shape
    rows = FFN_ROWS
    once = pl.Buffered(1)
    return pl.pallas_call(
        functools.partial(_proj_ffn_kernel, final_norm=final_norm),
        grid=(T // rows,),
        in_specs=[
            pl.BlockSpec((rows, D), lambda i: (i, 0)),
            pl.BlockSpec((1, D, D), lambda i: (0, 0, 0), pipeline_mode=once),
            pl.BlockSpec((rows, D), lambda i: (i, 0)),
            pl.BlockSpec((1, D), lambda i: (0, 0)),
            pl.BlockSpec((1, D, D_FF), lambda i: (layer, 0, 0), pipeline_mode=once),
            pl.BlockSpec((1, D, D_FF), lambda i: (layer, 0, 1), pipeline_mode=once),
            pl.BlockSpec((1, D_FF, D), lambda i: (layer, 0, 0), pipeline_mode=once),
            pl.BlockSpec((1, D), lambda i: (0, 0)),
        ],
        out_specs=pl.BlockSpec((rows, D), lambda i: (i, 0)),
        out_shape=jax.ShapeDtypeStruct((T, D), F32),
        compiler_params=_params(1),
        name="proj_ffn_final" if final_norm else "proj_ffn",
    )(a, wp, res, norm_w, wgu, wgu, wd, final_w)


def _mlstm_proj_kernel(x_ref, nw_ref, wqt_ref, wk_ref, wvt_ref, wot_ref, wgt_ref, bg_ref,
                       qt_ref, k_ref, vt_ref, ogt_ref, gt_ref):
    xn = _rms(x_ref[0], nw_ref[...]).astype(BF16)
    scale = MLSTM_QK_DIM ** -0.5
    nt = lambda w_ref: lax.dot_general(w_ref[...], xn, _NT, preferred_element_type=F32)
    qt_ref[0] = (nt(wqt_ref) * scale).astype(BF16)
    k_ref[0] = jnp.dot(xn, wk_ref[...], preferred_element_type=F32).astype(BF16)
    vt_ref[0] = nt(wvt_ref).astype(BF16)
    ogt_ref[0] = jax.nn.sigmoid(nt(wot_ref)).astype(BF16)
    gates = nt(wgt_ref) + bg_ref[...]
    gt_ref[0] = GATE_SOFTCAP * jnp.tanh(gates / GATE_SOFTCAP)


def _mlstm_proj(x, norm_w, wqt, wk, wvt, wot, wgt, bg):
    B, S, D = x.shape
    rows = PROJ_ROWS
    H, dk = MLSTM_HEADS, MLSTM_QK_DIM
    const = lambda b, i: (0, 0)
    tok = lambda w: pl.BlockSpec((1, rows, w), lambda b, i: (b, i, 0))
    tok_t = lambda w: pl.BlockSpec((1, w, rows), lambda b, i: (b, 0, i))
    return pl.pallas_call(
        _mlstm_proj_kernel,
        grid=(B, S // rows),
        in_specs=[
            tok(D),
            pl.BlockSpec((1, D), const),
            pl.BlockSpec((H * dk, D), const),
            pl.BlockSpec((D, H * dk), const),
            pl.BlockSpec((D, D), const),
            pl.BlockSpec((D, D), const),
            pl.BlockSpec((2 * H, D), const),
            pl.BlockSpec((2 * H, 1), const),
        ],
        out_specs=[tok_t(H * dk), tok(H * dk), tok_t(D), tok_t(D), tok_t(2 * H)],
        out_shape=[
            jax.ShapeDtypeStruct((B, H * dk, S), BF16),
            jax.ShapeDtypeStruct((B, S, H * dk), BF16),
            jax.ShapeDtypeStruct((B, D, S), BF16),
            jax.ShapeDtypeStruct((B, D, S), BF16),
            jax.ShapeDtypeStruct((B, 2 * H, S), F32),
        ],
        compiler_params=_params(2),
        name="mlstm_proj",
    )(x, norm_w, wqt, wk, wvt, wot, wgt, bg)


def _lane_scan(x, op, fill):
    lane = lax.broadcasted_iota(jnp.int32, (x.shape[0], LANES), 1)
    groups, carry = [], None
    for t in range(x.shape[1] // LANES):
        y = x[:, t * LANES:(t + 1) * LANES]
        sh = 1
        while sh < LANES:
            y = op(y, jnp.where(lane >= sh, pltpu.roll(y, sh, 1), fill))
            sh *= 2
        if carry is not None:
            y = op(y, carry)
        carry = y[:, LANES - 1:LANES]
        groups.append(y)
    return jnp.concatenate(groups, axis=1)


def _mlstm_core_kernel(qt_ref, k_ref, vt_ref, ogt_ref, gt_ref, hw_ref, y_ref, c_sc, m_sc):
    L = MLSTM_CHUNK
    H, dk, dv = MLSTM_HEADS, MLSTM_QK_DIM, MLSTM_V_DIM

    @pl.when(pl.program_id(1) == 0)
    def _():
        c_sc[...] = jnp.zeros_like(c_sc)
        m_sc[...] = jnp.zeros_like(m_sc)

    s_idx = lax.broadcasted_iota(jnp.int32, (L, L), 0)
    l_idx = lax.broadcasted_iota(jnp.int32, (L, L), 1)
    causal = s_idx <= l_idx
    feat = lax.broadcasted_iota(jnp.int32, (LANES, L), 0)
    one_rows = jnp.where(lax.broadcasted_iota(jnp.int32, (16, L), 0) == 0, 1.0, 0.0).astype(BF16)

    for c in range(qt_ref.shape[2] // L):
        cols = slice(c * L, (c + 1) * L)
        gates = gt_ref[0, :, cols]
        ipre, fpre = gates[0:H], gates[H:2 * H]
        logf = jnp.minimum(fpre, 0.0) - jnp.log1p(jnp.exp(-jnp.abs(fpre)))
        b = _lane_scan(logf, jnp.add, 0.0)
        u = ipre - b
        m_prev = m_sc[:, 0:1]
        big_m = jnp.maximum(_lane_scan(u, jnp.maximum, NEG_INF), m_prev)
        inter = jnp.exp(m_prev - big_m)
        floor = jnp.exp(-(b + big_m))
        m_last = big_m[:, L - 1:L]
        w = jnp.exp(u - m_last)
        decay = jnp.exp(m_prev - m_last)
        m_sc[...] = jnp.broadcast_to(b[:, L - 1:L] + m_last, (H, LANES))
        u_cols = jnp.concatenate([u * LOG2E, jnp.zeros((LANES - H, L), F32)], axis=0).T
        big_m2 = big_m * LOG2E

        for hp in range(H // 2):
            k2 = k_ref[0, cols, hp * LANES:(hp + 1) * LANES]
            qt2 = qt_ref[0, hp * LANES:(hp + 1) * LANES, cols]
            for a in range(2):
                h = 2 * hp + a
                qta = jnp.where((feat >= a * dk) & (feat < (a + 1) * dk), qt2, jnp.zeros_like(qt2))
                s_t = jnp.dot(k2, qta, preferred_element_type=F32)
                e_t = jnp.exp2(jnp.where(causal, u_cols[:, h:h + 1] - big_m2[h:h + 1], NEG_INF))
                v_ext = jnp.concatenate([vt_ref[0, h * dv:(h + 1) * dv, cols], one_rows], axis=0)
                state = c_sc[h]
                r = (inter[h:h + 1] * jnp.dot(state.astype(BF16), qta, preferred_element_type=F32)
                     + jnp.dot(v_ext, (s_t * e_t).astype(BF16), preferred_element_type=F32))
                hh = r[:dv] / jnp.maximum(jnp.abs(r[dv:dv + 1]), floor[h:h + 1])
                hn = hh * lax.rsqrt(jnp.mean(hh * hh, axis=0, keepdims=True) + RMS_EPS)
                hw = hw_ref[h * dv:(h + 1) * dv, :]
                hn = hn * jnp.concatenate([hw] * (L // LANES), axis=1)
                y_t = ogt_ref[0, h * dv:(h + 1) * dv, cols].astype(F32) * hn
                y_ref[0, cols, h * dv:(h + 1) * dv] = y_t.T.astype(BF16)

                wv = (v_ext.astype(F32) * w[h:h + 1]).astype(BF16)
                c_sc[h] = decay[h:h + 1] * state + jnp.dot(wv, k2, preferred_element_type=F32)


def _mlstm_core(qt, k, vt, ogt, gt, head_w):
    B, D, S = vt.shape
    rows = MLSTM_ROWS
    H, dk, dv = MLSTM_HEADS, MLSTM_QK_DIM, MLSTM_V_DIM
    tok_t = lambda w: pl.BlockSpec((1, w, rows), lambda b, i: (b, 0, i))
    return pl.pallas_call(
        _mlstm_core_kernel,
        grid=(B, S // rows),
        in_specs=[
            tok_t(H * dk),
            pl.BlockSpec((1, rows, H * dk), lambda b, i: (b, i, 0)),
            tok_t(D), tok_t(D), tok_t(2 * H),
            pl.BlockSpec((D, LANES), lambda b, i: (0, 0)),
        ],
        out_specs=pl.BlockSpec((1, rows, D), lambda b, i: (b, i, 0)),
        out_shape=jax.ShapeDtypeStruct((B, S, D), BF16),
        scratch_shapes=[pltpu.VMEM((H, dv + 16, LANES), F32), pltpu.VMEM((H, LANES), F32)],
        compiler_params=_params(2),
        name="mlstm_core",
    )(qt, k, vt, ogt, gt, head_w)


def _rotary_tables(seq):
    pos = jnp.arange(seq, dtype=F32)
    inv_freq = ROPE_THETA ** (-jnp.arange(0, ROT_DIM, 2, dtype=F32) / ROT_DIM)
    ang = pos[:, None] * inv_freq[None, :]
    cos, sin = jnp.cos(ang), jnp.sin(ang)
    d = jnp.arange(LANES) % ATTN_HEAD_DIM
    cos_l, sin_l = cos[:, d % ROT_HALF], sin[:, d % ROT_HALF]
    cn = jnp.where(d < ROT_DIM, cos_l, 1.0)
    sa = jnp.where(d < ROT_HALF, -sin_l, 0.0)
    sb = jnp.where((d >= ROT_HALF) & (d < ROT_DIM), sin_l, 0.0)
    return cn, sa, sb, cos.T, sin.T


def kernel(x, attn_norm, attn_w_qkv, attn_w_o, mlstm_norm, mlstm_w_in, mlstm_b_gates, mlstm_head_norm,
           mlstm_w_out, ffn_norm, ffn_w_gate_up, ffn_w_down, final_norm):
    B, S, D = x.shape
    T = B * S
    H, dk = MLSTM_HEADS, MLSTM_QK_DIM
    row = lambda w: w.reshape(1, -1).astype(F32)

    cn, sa, sb, cost, sint = _rotary_tables(S)
    wqkv = attn_w_qkv[0].astype(BF16)
    qt, k, vt, kmean = _attn_proj(x, row(attn_norm[0]), wqkv[:, :D].T, wqkv[:, D:2 * D], wqkv[:, 2 * D:].T,
                                  cn, sa, sb, cost, sint)
    o = _moba(qt, k, vt, kmean.reshape(B, S // MOBA_BLOCK, D))
    wgu, wd = ffn_w_gate_up.astype(BF16), ffn_w_down.astype(BF16)
    h = _proj_ffn(o.reshape(T, D), attn_w_o.astype(BF16), x.reshape(T, D), row(ffn_norm[0]), wgu, wd,
                  row(final_norm), layer=0, final_norm=False)

    w_in = mlstm_w_in[0].astype(BF16)
    q_end, k_end, v_end, o_end = H * dk, 2 * H * dk, 2 * H * dk + D, 2 * H * dk + 2 * D
    qt, k, vt, ogt, gt = _mlstm_proj(
        h.reshape(B, S, D), row(mlstm_norm[0]), w_in[:, :q_end].T, w_in[:, q_end:k_end],
        w_in[:, k_end:v_end].T, w_in[:, v_end:o_end].T, w_in[:, o_end:].T,
        mlstm_b_gates[0].reshape(2 * H, 1).astype(F32))
    head_w = jnp.broadcast_to(mlstm_head_norm[0].astype(F32).reshape(D, 1), (D, LANES))
    y = _mlstm_core(qt, k, vt, ogt, gt, head_w)
    out = _proj_ffn(y.reshape(T, D), mlstm_w_out.astype(BF16), h, row(ffn_norm[1]), wgu, wd,
                    row(final_norm), layer=1, final_norm=True)
    return out.reshape(B, S, D)
```
